```python
import math
import jax, jax.numpy as jnp
from jax import lax
import numpy as np

D_MODEL = 2048
BATCH = 8
SEQ = 4096
DEPTH = 2
DEC_BATCH = 16
DEC_SEQ = 2048
PAST_LEN = 128

HEAD_DIM = 128
ROPE_THETA = 500000.0
ROT_FRAC_DIV = 4
Q_BLOCK = 128
EPS = 1e-6
NEG = -1e30
GRID_W = 64
WIN_ROWS = 8
WIN_COLS = 16
NA_HEADS = 4
DIL_WINDOWS = (128, 512, 2048)
DILATIONS = (1, 4, 16)
DIL_GROUPS = 3
DIL_HEADS = 2
DIL_SPAN = DIL_WINDOWS[0] // (2 * DILATIONS[0])
DIFF_HEADS = 6
DIFF_QK_DIM = 64
DIFF_V_DIM = 2 * DIFF_QK_DIM
D_FF = 4 * D_MODEL
NA_W = NA_HEADS * HEAD_DIM
DIL_W = DIL_GROUPS * DIL_HEADS * HEAD_DIM
DIL_OUT_W = DIL_HEADS * HEAD_DIM
DIFF_QK_W = DIFF_HEADS * 2 * DIFF_QK_DIM
DIFF_V_W = DIFF_HEADS * DIFF_V_DIM
IN_COLS = 3 * NA_W + 3 * DIL_W + 2 * DIFF_QK_W + DIFF_V_W

kernel_name = "hybrid_na_dilated_diff_encoder"


def rms_norm(x, g):
    xf = x.astype(jnp.float32)
    y = xf * lax.rsqrt(jnp.mean(xf * xf, axis=-1, keepdims=True) + EPS)
    return (y * g.astype(jnp.float32)).astype(x.dtype)


def partial_rope(x, pos):
    rot = x.shape[-1] // ROT_FRAC_DIV
    half = rot // 2
    inv = ROPE_THETA ** (-jnp.arange(half, dtype=jnp.float32) / half)
    ang = pos.astype(jnp.float32)[:, None] * inv[None, :]
    cos, sin = jnp.cos(ang), jnp.sin(ang)
    xr = x[..., :rot].astype(jnp.float32)
    x1, x2 = xr[..., :half], xr[..., half:]
    xr = jnp.concatenate([x1 * cos - x2 * sin, x2 * cos + x1 * sin], axis=-1)
    return jnp.concatenate([xr.astype(x.dtype), x[..., rot:]], axis=-1)


def neighbourhood_attention(q, k, v, rpb):
    B, H, S, dh = q.shape
    rows = S // GRID_W
    kh, kw = min(WIN_ROWS, rows), WIN_COLS
    qg = q.reshape(B, H, rows, GRID_W, dh)
    kg = k.reshape(B, H, rows, GRID_W, dh)
    vg = v.reshape(B, H, rows, GRID_W, dh)
    r_ids = jnp.arange(rows)
    c_ids = jnp.arange(GRID_W)
    row_start = jnp.clip(r_ids - kh // 2, 0, rows - kh)
    col_start = jnp.clip(c_ids - kw // 2, 0, GRID_W - kw)
    col_idx = col_start[:, None] + jnp.arange(kw)[None, :]
    dc = col_idx - c_ids[:, None]
    bias_c = rpb[:, :, dc + WIN_COLS - 1]
    scale = dh ** -0.5

    def one_row(rr):
        rs = row_start[rr]
        q_row = lax.dynamic_index_in_dim(qg, rr, axis=2, keepdims=False)
        k_band = lax.dynamic_slice_in_dim(kg, rs, kh, axis=2)
        v_band = lax.dynamic_slice_in_dim(vg, rs, kh, axis=2)
        k_sel = k_band[:, :, :, col_idx]
        v_sel = v_band[:, :, :, col_idx]
        s = jnp.einsum('bhwd,bhrwkd->bhwrk', q_row, k_sel,
                       preferred_element_type=jnp.float32) * scale
        dr = rs + jnp.arange(kh) - rr
        bias = bias_c[:, dr + WIN_ROWS - 1].transpose(0, 2, 1, 3)
        s = s + bias[None].astype(jnp.float32)
        p = jax.nn.softmax(s.reshape(B, H, GRID_W, kh * kw), axis=-1)
        p = p.reshape(B, H, GRID_W, kh, kw).astype(v.dtype)
        return jnp.einsum('bhwrk,bhrwkd->bhwd', p, v_sel)

    out = lax.map(one_row, r_ids)
    return out.transpose(1, 2, 0, 3, 4).reshape(B, H, S, dh)


def dilated_attention(q, k, v):
    B, G, Hg, S, dh = q.shape
    nblk = S // Q_BLOCK
    offs = jnp.arange(-DIL_SPAN, DIL_SPAN + 1)
    dil = jnp.array(DILATIONS, dtype=jnp.int32)
    scale = dh ** -0.5
    take_g = jax.vmap(lambda kk, ii: jnp.take(kk, ii, axis=2), in_axes=(1, 0), out_axes=1)

    def one_block(i):
        t = i * Q_BLOCK + jnp.arange(Q_BLOCK)
        idx = t[None, :, None] + dil[:, None, None] * offs[None, None, :]
        valid = (idx >= 0) & (idx < S)
        idxc = jnp.clip(idx, 0, S - 1)
        qb = lax.dynamic_slice_in_dim(q, i * Q_BLOCK, Q_BLOCK, axis=3)
        kb = take_g(k, idxc)
        vb = take_g(v, idxc)
        s = jnp.einsum('bghqd,bghqnd->bghqn', qb, kb,
                       preferred_element_type=jnp.float32) * scale
        s = jnp.where(valid[None, :, None], s, NEG)
        lse = jax.nn.logsumexp(s, axis=-1)
        p = jnp.exp(s - lse[..., None])
        o = jnp.einsum('bghqn,bghqnd->bghqd', p.astype(v.dtype), vb,
                       preferred_element_type=jnp.float32)
        alpha = jax.nn.softmax(lse, axis=1)
        return jnp.sum(alpha[..., None] * o, axis=1).astype(v.dtype)

    out = lax.map(one_block, jnp.arange(nblk))
    return out.transpose(1, 2, 0, 3, 4).reshape(B, Hg, S, dh)


def diff_attention(q, k, v, lam):
    B, H, _, S, dc = q.shape
    nblk = S // Q_BLOCK
    scale = dc ** -0.5
    qblocks = jnp.moveaxis(q.reshape(B, H, 2, nblk, Q_BLOCK, dc), 3, 0)

    def one_block(qb):
        s = jnp.einsum('bhcqd,bhckd->bhcqk', qb, k,
                       preferred_element_type=jnp.float32) * scale
        p = jax.nn.softmax(s, axis=-1)
        a = p[:, :, 0] - lam * p[:, :, 1]
        return jnp.einsum('bhqk,bhkd->bhqd', a.astype(v.dtype), v)

    out = lax.map(one_block, qblocks)
    return out.transpose(1, 2, 0, 3, 4).reshape(B, H, S, v.shape[-1])


def encoder_layer(x, c, layer_idx, norm1_g, w_ada, b_ada, w_in, qn_a, kn_a, rpb_a,
                  qn_b, kn_b, qn_c, kn_c, lam_q1, lam_k1, lam_q2, lam_k2, subln_c,
                  w_br_a, w_br_b, w_br_c, w_gate, b_gate, w_out, norm2_g, w_ff1, w_ff2):
    B, S, _ = x.shape
    pos = jnp.arange(S)
    mod = (jax.nn.silu(c) @ w_ada + b_ada).astype(x.dtype)[:, None, :]
    sh1, sc1, g1, sh2, sc2, g2 = jnp.split(mod, 6, axis=-1)

    h = rms_norm(x, norm1_g) * (1 + sc1) + sh1
    z = h @ w_in
    widths = [NA_W, NA_W, NA_W, DIL_W, DIL_W, DIL_W, DIFF_QK_W, DIFF_QK_W]
    splits = [int(s) for s in np.cumsum(widths)]
    qa, ka, va, qb, kb, vb, qc, kc, vc = jnp.split(z, splits, axis=-1)

    to_na = lambda t: t.reshape(B, S, NA_HEADS, HEAD_DIM).transpose(0, 2, 1, 3)
    oa = neighbourhood_attention(rms_norm(to_na(qa), qn_a), rms_norm(to_na(ka), kn_a),
                                 to_na(va), rpb_a)

    to_dil = lambda t: t.reshape(B, S, DIL_GROUPS, DIL_HEADS, HEAD_DIM).transpose(0, 2, 3, 1, 4)
    ob = dilated_attention(partial_rope(rms_norm(to_dil(qb), qn_b), pos),
                           partial_rope(rms_norm(to_dil(kb), kn_b), pos),
                           to_dil(vb))

    to_diff = lambda t: t.reshape(B, S, DIFF_HEADS, 2, DIFF_QK_DIM).transpose(0, 2, 3, 1, 4)
    lam_init = 0.8 - 0.6 * math.exp(-0.3 * layer_idx)
    lam = (jnp.exp(jnp.sum(lam_q1.astype(jnp.float32) * lam_k1.astype(jnp.float32)))
           - jnp.exp(jnp.sum(lam_q2.astype(jnp.float32) * lam_k2.astype(jnp.float32)))
           + lam_init)
    vcs = vc.reshape(B, S, DIFF_HEADS, DIFF_V_DIM).transpose(0, 2, 1, 3)
    oc = diff_attention(partial_rope(rms_norm(to_diff(qc), qn_c), pos),
                        partial_rope(rms_norm(to_diff(kc), kn_c), pos), vcs, lam)
    oc = rms_norm(oc, subln_c) * (1.0 - lam_init)

    ya = oa.transpose(0, 2, 1, 3).reshape(B, S, NA_W) @ w_br_a
    yb = ob.transpose(0, 2, 1, 3).reshape(B, S, DIL_OUT_W) @ w_br_b
    yc = oc.transpose(0, 2, 1, 3).reshape(B, S, DIFF_V_W) @ w_br_c
    ga, gb, gc = jnp.split(jax.nn.sigmoid(h @ w_gate + b_gate), 3, axis=-1)
    x = x + g1 * ((ga * ya + gb * yb + gc * yc) @ w_out)

    h2 = rms_norm(x, norm2_g) * (1 + sc2) + sh2
    f = jnp.square(jax.nn.relu(h2 @ w_ff1)) @ w_ff2
    return x + g2 * f


def setup_inputs(seed: int = 0) -> dict:
    key = jax.random.key(seed)
    ks = jax.random.split(key, 29)

    def nrm(i, shape, scale):
        return jax.random.normal(ks[i], shape, jnp.float32) * scale

    D = D_MODEL
    return {
        "x_prompt": nrm(0, (BATCH, SEQ, D), 1.0),
        "x_sample": nrm(1, (DEC_BATCH, DEC_SEQ, D), 1.0),
        "c_prompt": nrm(2, (BATCH, D), 1.0),
        "c_sample": nrm(3, (DEC_BATCH, D), 1.0),
        "norm1_g": 1.0 + nrm(4, (DEPTH, D), 0.05),
        "w_ada": nrm(5, (DEPTH, D, 6 * D), 0.5 * D ** -0.5),
        "b_ada": nrm(6, (DEPTH, 6 * D), 0.02),
        "w_in": nrm(7, (DEPTH, D, IN_COLS), D ** -0.5),
        "qn_a": 1.0 + nrm(8, (DEPTH, HEAD_DIM), 0.05),
        "kn_a": 1.0 + nrm(9, (DEPTH, HEAD_DIM), 0.05),
        "rpb_a": nrm(10, (DEPTH, NA_HEADS, 2 * WIN_ROWS - 1, 2 * WIN_COLS - 1), 0.5),
        "qn_b": 1.0 + nrm(11, (DEPTH, HEAD_DIM), 0.05),
        "kn_b": 1.0 + nrm(12, (DEPTH, HEAD_DIM), 0.05),
        "qn_c": 1.0 + nrm(13, (DEPTH, DIFF_QK_DIM), 0.05),
        "kn_c": 1.0 + nrm(14, (DEPTH, DIFF_QK_DIM), 0.05),
        "lam_q1": nrm(15, (DEPTH, DIFF_QK_DIM), 0.1),
        "lam_k1": nrm(16, (DEPTH, DIFF_QK_DIM), 0.1),
        "lam_q2": nrm(17, (DEPTH, DIFF_QK_DIM), 0.1),
        "lam_k2": nrm(18, (DEPTH, DIFF_QK_DIM), 0.1),
        "subln_c": 1.0 + nrm(19, (DEPTH, DIFF_V_DIM), 0.05),
        "w_br_a": nrm(20, (DEPTH, NA_W, D), NA_W ** -0.5),
        "w_br_b": nrm(21, (DEPTH, DIL_OUT_W, D), DIL_OUT_W ** -0.5),
        "w_br_c": nrm(22, (DEPTH, DIFF_V_W, D), DIFF_V_W ** -0.5),
        "w_gate": nrm(23, (DEPTH, D, 3 * D), D ** -0.5),
        "b_gate": nrm(24, (DEPTH, 3 * D), 0.02),
        "w_out": nrm(25, (DEPTH, D, D), D ** -0.5),
        "norm2_g": 1.0 + nrm(26, (DEPTH, D), 0.05),
        "w_ff1": nrm(27, (DEPTH, D, D_FF), D ** -0.5),
        "w_ff2": nrm(28, (DEPTH, D_FF, D), D_FF ** -0.5),
    }


def reference(x_prompt, x_sample, c_prompt, c_sample, norm1_g, w_ada, b_ada, w_in,
              qn_a, kn_a, rpb_a, qn_b, kn_b, qn_c, kn_c, lam_q1, lam_k1, lam_q2, lam_k2,
              subln_c, w_br_a, w_br_b, w_br_c, w_gate, b_gate, w_out, norm2_g, w_ff1, w_ff2):
    y_prompt, y_sample = x_prompt, x_sample
    for l in range(DEPTH):
        p = (norm1_g[l], w_ada[l], b_ada[l], w_in[l], qn_a[l], kn_a[l], rpb_a[l],
             qn_b[l], kn_b[l], qn_c[l], kn_c[l], lam_q1[l], lam_k1[l], lam_q2[l], lam_k2[l],
             subln_c[l], w_br_a[l], w_br_b[l], w_br_c[l], w_gate[l], b_gate[l], w_out[l],
             norm2_g[l], w_ff1[l], w_ff2[l])
        y_prompt = encoder_layer(y_prompt, c_prompt, l, *p)
        y_sample = encoder_layer(y_sample, c_sample, l, *p)
    return (y_prompt, y_sample)
```

```python
import functools
import math

import jax
import jax.numpy as jnp
from jax import lax
from jax.experimental import pallas as pl
from jax.experimental.pallas import tpu as pltpu

F32 = jnp.float32
BF16 = jnp.bfloat16

LANES = 128
HEAD_DIM = 128
ROPE_THETA = 500000.0
EPS = 1e-6
NEG = -1e30
GRID_W = 64
WIN_ROWS = 8
WIN_COLS = 16
NA_HEADS = 4
DILATIONS = (1, 4, 16)
DIL_HEADS = 2
DIL_SPAN = 64
DIFF_HEADS = 6
DIFF_QK_DIM = 64
IN_COLS = 6144
QA, KA, VA, QB, KB, VB, QC, KC, VC = 0, 4, 8, 12, 18, 24, 30, 36, 42
N_SLABS = IN_COLS // LANES

VMEM_LIMIT = 56 * 1024 * 1024


def _cparams(sem, vmem=VMEM_LIMIT):
    return pltpu.CompilerParams(dimension_semantics=sem, vmem_limit_bytes=vmem)


def _pick(n, pref):
    t = min(n, pref)
    while n % t:
        t //= 2
    return t


def _rms(x, width):
    return x * lax.rsqrt(jnp.sum(x * x, axis=-1, keepdims=True) * (1.0 / width) + EPS)


def _ada_kernel(c_ref, w_ref, b_ref, o_ref):
    c = c_ref[...]
    a = c * (1.0 / (1.0 + jnp.exp(-c)))
    o_ref[0] = jnp.dot(a.astype(BF16), w_ref[0].astype(BF16), preferred_element_type=F32) + b_ref[0]


def _ada(c_all, w_ada, b_ada):
    depth, d, n = w_ada.shape
    nb = c_all.shape[0]
    tn = _pick(n, 1024)
    return pl.pallas_call(
        _ada_kernel,
        grid=(depth, n // tn),
        in_specs=[
            pl.BlockSpec((nb, d), lambda l, j: (0, 0)),
            pl.BlockSpec((1, d, tn), lambda l, j: (l, 0, j)),
            pl.BlockSpec((1, 1, tn), lambda l, j: (l, 0, j)),
        ],
        out_specs=pl.BlockSpec((1, nb, tn), lambda l, j: (l, 0, j)),
        out_shape=jax.ShapeDtypeStruct((depth, nb, n), F32),
        compiler_params=_cparams(("parallel", "parallel")),
    )(c_all, w_ada, b_ada.reshape(depth, 1, n))


NORM_ROWS = 64


def _modnorm_to(h_scr, x_ref, g_ref, sc_ref, sh_ref):
    tm, d = x_ref.shape
    g = g_ref[...]
    sc = 1.0 + sc_ref[...]
    sh = sh_ref[...]

    def body(i, carry):
        sl = pl.ds(pl.multiple_of(i * NORM_ROWS, NORM_ROWS), NORM_ROWS)
        x = x_ref[sl, :]
        h_scr[sl, :] = (_rms(x, d) * g * sc + sh).astype(BF16)
        return carry

    lax.fori_loop(0, tm // NORM_ROWS, body, 0)


def _inproj_kernel(x_ref, g_ref, sc_ref, sh_ref, w_ref, z_ref, h_scr):
    @pl.when(pl.program_id(1) == 0)
    def _():
        _modnorm_to(h_scr, x_ref, g_ref, sc_ref, sh_ref)

    acc = jnp.dot(h_scr[...], w_ref[...], preferred_element_type=F32)
    for c in range(z_ref.shape[0]):
        z_ref[c] = acc[:, c * LANES:(c + 1) * LANES].astype(BF16)


def _inproj(x2, seq, g, sc, sh, w):
    t, d = x2.shape
    n = w.shape[1]
    tm = _pick(seq, 1024)
    tn = _pick(n, 1024)
    per = seq // tm
    return pl.pallas_call(
        _inproj_kernel,
        grid=(t // tm, n // tn),
        in_specs=[
            pl.BlockSpec((tm, d), lambda i, j: (i, 0)),
            pl.BlockSpec((1, d), lambda i, j: (0, 0)),
            pl.BlockSpec((None, 1, d), lambda i, j: (i // per, 0, 0)),
            pl.BlockSpec((None, 1, d), lambda i, j: (i // per, 0, 0)),
            pl.BlockSpec((d, tn), lambda i, j: (0, j)),
        ],
        out_specs=pl.BlockSpec((tn // LANES, tm, LANES), lambda i, j: (j, i, 0)),
        out_shape=jax.ShapeDtypeStruct((n // LANES, t, LANES), BF16),
        scratch_shapes=[pltpu.VMEM((tm, d), BF16)],
        compiler_params=_cparams(("parallel", "arbitrary")),
    )(x2, g, sc, sh, w)


def _na_bias(rpb):
    c = jnp.arange(GRID_W)
    cs = jnp.clip(c - WIN_COLS // 2, 0, GRID_W - WIN_COLS)
    kc = jnp.arange(GRID_W)
    inwin = (kc[None, :] >= cs[:, None]) & (kc[None, :] < cs[:, None] + WIN_COLS)
    dc = jnp.clip(kc[None, :] - c[:, None] + WIN_COLS - 1, 0, 2 * WIN_COLS - 2)
    delta = jnp.arange(WIN_ROWS)
    kr = jnp.arange(WIN_ROWS)
    dr = kr[None, :] - delta[:, None] + WIN_ROWS - 1
    b = rpb[:, dr[:, None, :, None], dc[None, :, None, :]]
    b = jnp.where(inwin[None, None, :, None, :], b.astype(F32), NEG)
    return b.reshape(rpb.shape[0], WIN_ROWS, GRID_W, WIN_ROWS * GRID_W)


def _na_kernel(q_ref, k_ref, v_ref, bias_ref, qn_ref, kn_ref, o_ref, kn_scr, *, rows):
    band = WIN_ROWS * GRID_W
    scale = HEAD_DIM ** -0.5
    kg = kn_ref[...]
    qg = qn_ref[...] * scale

    def knorm(i, carry):
        sl = pl.ds(pl.multiple_of(i * 256, 256), 256)
        kn_scr[sl, :] = (_rms(k_ref[sl, :].astype(F32), HEAD_DIM) * kg).astype(BF16)
        return carry

    lax.fori_loop(0, rows * GRID_W // 256, knorm, 0)

    def body(r, carry):
        rs = jnp.clip(r - WIN_ROWS // 2, 0, rows - WIN_ROWS)
        qsl = pl.ds(pl.multiple_of(r * GRID_W, GRID_W), GRID_W)
        ksl = pl.ds(pl.multiple_of(rs * GRID_W, GRID_W), band)
        q = (_rms(q_ref[qsl, :].astype(F32), HEAD_DIM) * qg).astype(BF16)
        s = lax.dot_general(q, kn_scr[ksl, :], (((1,), (1,)), ((), ())), preferred_element_type=F32)
        s = s + bias_ref[r - rs]
        m = jnp.max(s, axis=-1, keepdims=True)
        p = jnp.exp(s - m)
        l = jnp.sum(p, axis=-1, keepdims=True)
        o = jnp.dot(p.astype(BF16), v_ref[ksl, :], preferred_element_type=F32) / l
        o_ref[qsl, :] = o.astype(BF16)
        return carry

    lax.fori_loop(0, rows, body, 0)


def _na(z4, bias, qn, kn):
    _, b, s, _ = z4.shape
    rows = s // GRID_W
    assert rows >= WIN_ROWS and s % 256 == 0
    slab = lambda off: pl.BlockSpec((None, None, s, LANES), lambda bi, h: (off + h, bi, 0, 0))
    return pl.pallas_call(
        functools.partial(_na_kernel, rows=rows),
        grid=(b, NA_HEADS),
        in_specs=[
            slab(QA), slab(KA), slab(VA),
            pl.BlockSpec((None, WIN_ROWS, GRID_W, WIN_ROWS * GRID_W), lambda bi, h: (h, 0, 0, 0)),
            pl.BlockSpec((1, LANES), lambda bi, h: (0, 0)),
            pl.BlockSpec((1, LANES), lambda bi, h: (0, 0)),
        ],
        out_specs=pl.BlockSpec((None, None, s, LANES), lambda bi, h: (h, bi, 0, 0)),
        out_shape=jax.ShapeDtypeStruct((NA_HEADS, b, s, LANES), BF16),
        scratch_shapes=[pltpu.VMEM((s, LANES), BF16)],
        compiler_params=_cparams(("parallel", "parallel")),
    )(z4, z4, z4, bias, qn, kn)


def _rope_tables(seq, half, starts):
    inv = ROPE_THETA ** (-jnp.arange(half, dtype=F32) / half)
    ang = jnp.arange(seq, dtype=F32)[:, None] * inv[None, :]
    cos, sin = jnp.cos(ang), jnp.sin(ang)
    c = jnp.ones((seq, LANES), F32)
    up = jnp.zeros((seq, LANES), F32)
    dn = jnp.zeros((seq, LANES), F32)
    for st in starts:
        c = c.at[:, st:st + half].set(cos).at[:, st + half:st + 2 * half].set(cos)
        up = up.at[:, st:st + half].set(-sin)
        dn = dn.at[:, st + half:st + 2 * half].set(sin)
    return c, up, dn


def _rope(x, c, up, dn, half):
    return x * c + pltpu.roll(x, LANES - half, 1) * up + pltpu.roll(x, half, 1) * dn


DIL_HALF = HEAD_DIM // 4 // 2


def _dil_kernel(q_ref, k_ref, v_ref, c_ref, up_ref, dn_ref, qn_ref, kn_ref, o_ref, lse_ref, kn_scr,
                *, n, tq, kw):
    scale = HEAD_DIM ** -0.5
    kg = kn_ref[...]
    qg = qn_ref[...]
    rk = _pick(n, 256)

    for hh in range(DIL_HEADS):
        def knorm(i, carry, hh=hh):
            sl = pl.ds(pl.multiple_of(i * rk, rk), rk)
            k = _rms(k_ref[hh, sl, :].astype(F32), HEAD_DIM) * kg
            kn_scr[hh, sl, :] = _rope(k, c_ref[sl, :], up_ref[sl, :], dn_ref[sl, :], DIL_HALF).astype(BF16)
            return carry

        lax.fori_loop(0, n // rk, knorm, 0)

    for hh in range(DIL_HEADS):
        def body(i, carry, hh=hh):
            q0 = pl.multiple_of(i * tq, tq)
            ks = pl.multiple_of(jnp.clip(q0 - DIL_SPAN, 0, n - kw), DIL_SPAN)
            qsl = pl.ds(q0, tq)
            ksl = pl.ds(ks, kw)
            q = _rms(q_ref[hh, qsl, :].astype(F32), HEAD_DIM) * qg
            q = _rope(q, c_ref[qsl, :], up_ref[qsl, :], dn_ref[qsl, :], DIL_HALF) * scale
            s = lax.dot_general(q.astype(BF16), kn_scr[hh, ksl, :], (((1,), (1,)), ((), ())),
                                preferred_element_type=F32)
            qpos = q0 + lax.broadcasted_iota(jnp.int32, (tq, kw), 0)
            kpos = ks + lax.broadcasted_iota(jnp.int32, (tq, kw), 1)
            s = jnp.where(jnp.abs(kpos - qpos) <= DIL_SPAN, s, NEG)
            m = jnp.max(s, axis=-1, keepdims=True)
            p = jnp.exp(s - m)
            l = jnp.sum(p, axis=-1, keepdims=True)
            o = jnp.dot(p.astype(BF16), v_ref[hh, ksl, :], preferred_element_type=F32) / l
            o_ref[hh, qsl, :] = o.astype(BF16)
            half_l = LANES // DIL_HEADS
            lse_ref[qsl, hh * half_l:(hh + 1) * half_l] = jnp.broadcast_to(m + jnp.log(l), (tq, half_l))
            return carry

        lax.fori_loop(0, n // tq, body, 0)


def _dil(z4, g, tabs, qn, kn):
    d = DILATIONS[g]
    nsl, b, s, _ = z4.shape
    n = s // d
    tq = min(128, n)
    kw = min(2 * tq, n)
    assert n % tq == 0 and n >= DIL_SPAN and tq >= DIL_SPAN
    zv = z4.reshape(nsl, b, n, d * LANES)
    tabs = [t.reshape(n, d * LANES) for t in tabs]
    pair = lambda off: pl.BlockSpec((DIL_HEADS, None, n, LANES), lambda bi, r: ((off + 2 * g) // 2, bi, 0, r))
    tab = pl.BlockSpec((n, LANES), lambda bi, r: (0, r))
    vec = pl.BlockSpec((1, LANES), lambda bi, r: (0, 0))
    o, lse = pl.pallas_call(
        functools.partial(_dil_kernel, n=n, tq=tq, kw=kw),
        grid=(b, d),
        in_specs=[pair(QB), pair(KB), pair(VB), tab, tab, tab, vec, vec],
        out_specs=[
            pl.BlockSpec((DIL_HEADS, None, n, LANES), lambda bi, r: (0, bi, 0, r)),
            pl.BlockSpec((None, n, LANES), lambda bi, r: (bi, 0, r)),
        ],
        out_shape=[
            jax.ShapeDtypeStruct((DIL_HEADS, b, n, d * LANES), BF16),
            jax.ShapeDtypeStruct((b, n, d * LANES), F32),
        ],
        scratch_shapes=[pltpu.VMEM((DIL_HEADS, n, LANES), BF16)],
        compiler_params=_cparams(("parallel", "parallel")),
    )(zv, zv, zv, *tabs, qn, kn)
    return o.reshape(DIL_HEADS, b * s, LANES), lse.reshape(b * s, LANES)


DIFF_HALF = DIFF_QK_DIM // 4 // 2
DIFF_TK = 512


def _rms_halves(x):
    lo = lax.broadcasted_iota(jnp.int32, x.shape, 1) < DIFF_QK_DIM
    xx = x * x
    s_lo = jnp.sum(jnp.where(lo, xx, 0.0), axis=-1, keepdims=True)
    s_hi = jnp.sum(jnp.where(lo, 0.0, xx), axis=-1, keepdims=True)
    r = lax.rsqrt(jnp.where(lo, s_lo, s_hi) * (1.0 / DIFF_QK_DIM) + EPS)
    return x * r


def _diff_kernel(q_ref, k_ref, v_ref, c_ref, up_ref, dn_ref, qn_ref, kn_ref, lam_ref, sub_ref, o_ref, kn_scr,
                 *, seq, tq, lam_init):
    scale = DIFF_QK_DIM ** -0.5
    tk = min(DIFF_TK, seq)

    @pl.when(pl.program_id(2) == 0)
    def _():
        kg = kn_ref[...]

        def knorm(i, carry):
            sl = pl.ds(pl.multiple_of(i * 256, 256), 256)
            k = _rms_halves(k_ref[sl, :].astype(F32)) * kg
            kn_scr[sl, :] = _rope(k, c_ref[sl, :], up_ref[sl, :], dn_ref[sl, :], DIFF_HALF).astype(BF16)
            return carry

        lax.fori_loop(0, seq // 256, knorm, 0)

    lv = lam_ref[...]
    lam = (jnp.exp(jnp.sum(lv[0:1] * lv[1:2], axis=-1, keepdims=True))
           - jnp.exp(jnp.sum(lv[2:3] * lv[3:4], axis=-1, keepdims=True)) + lam_init)

    q0 = pl.multiple_of(pl.program_id(2) * tq, tq)
    qsl = pl.ds(q0, tq)
    q = _rms_halves(q_ref[...].astype(F32)) * qn_ref[...]
    q = _rope(q, c_ref[qsl, :], up_ref[qsl, :], dn_ref[qsl, :], DIFF_HALF) * scale
    lo = lax.broadcasted_iota(jnp.int32, q.shape, 1) < DIFF_QK_DIM
    q2 = jnp.concatenate([jnp.where(lo, q, 0.0), jnp.where(lo, 0.0, q)], axis=0).astype(BF16)

    def body(j, carry):
        m, l, acc = carry
        ksl = pl.ds(pl.multiple_of(j * tk, tk), tk)
        s = lax.dot_general(q2, kn_scr[ksl, :], (((1,), (1,)), ((), ())), preferred_element_type=F32)
        m_new = jnp.maximum(m, jnp.max(s, axis=-1, keepdims=True))
        alpha = jnp.exp(m - m_new)
        p = jnp.exp(s - m_new)
        l = alpha * l + jnp.sum(p, axis=-1, keepdims=True)
        acc = alpha * acc + jnp.dot(p.astype(BF16), v_ref[ksl, :], preferred_element_type=F32)
        return m_new, l, acc

    m0 = jnp.full((2 * tq, 1), -jnp.inf, F32)
    l0 = jnp.zeros((2 * tq, 1), F32)
    a0 = jnp.zeros((2 * tq, HEAD_DIM), F32)
    _, l, acc = lax.fori_loop(0, seq // tk, body, (m0, l0, a0))
    on = acc / l
    o = on[:tq] - lam * on[tq:]
    o_ref[...] = (_rms(o, HEAD_DIM) * sub_ref[...] * (1.0 - lam_init)).astype(BF16)


def _diff(z4, tabs, qn, kn, lamv, sub, lam_init):
    _, b, s, _ = z4.shape
    tq = min(128, s)
    assert s % 256 == 0
    kv = lambda off: pl.BlockSpec((None, None, s, LANES), lambda bi, h, qi: (off + h, bi, 0, 0))
    tab = pl.BlockSpec((s, LANES), lambda bi, h, qi: (0, 0))
    vec = pl.BlockSpec((1, LANES), lambda bi, h, qi: (0, 0))
    return pl.pallas_call(
        functools.partial(_diff_kernel, seq=s, tq=tq, lam_init=lam_init),
        grid=(b, DIFF_HEADS, s // tq),
        in_specs=[
            pl.BlockSpec((None, None, tq, LANES), lambda bi, h, qi: (QC + h, bi, qi, 0)),
            kv(KC), kv(VC), tab, tab, tab, vec, vec,
            pl.BlockSpec((4, DIFF_QK_DIM), lambda bi, h, qi: (0, 0)),
            vec,
        ],
        out_specs=pl.BlockSpec((None, None, tq, LANES), lambda bi, h, qi: (h, bi, qi, 0)),
        out_shape=jax.ShapeDtypeStruct((DIFF_HEADS, b, s, LANES), BF16),
        scratch_shapes=[pltpu.VMEM((s, LANES), BF16)],
        compiler_params=_cparams(("parallel", "parallel", "arbitrary")),
    )(z4, z4, z4, *tabs, qn, kn, lamv, sub)


def _merge_kernel(x_ref, g_ref, sc_ref, sh_ref, oa_ref, ob0_ref, ob1_ref, ob2_ref, l0_ref, l1_ref, l2_ref,
                  oc_ref, wa_ref, wb_ref, wc_ref, wga_ref, wgb_ref, wgc_ref, bga_ref, bgb_ref, bgc_ref,
                  m_ref, h_scr, ob_scr):
    tm = x_ref.shape[0]

    @pl.when(pl.program_id(1) == 0)
    def _():
        _modnorm_to(h_scr, x_ref, g_ref, sc_ref, sh_ref)
        l0, l1, l2 = l0_ref[...], l1_ref[...], l2_ref[...]
        mx = jnp.maximum(jnp.maximum(l0, l1), l2)
        e0, e1, e2 = jnp.exp(l0 - mx), jnp.exp(l1 - mx), jnp.exp(l2 - mx)
        den = e0 + e1 + e2
        half_l = LANES // DIL_HEADS
        for hh in range(DIL_HEADS):
            col = slice(hh * half_l, hh * half_l + 1)
            mix = ((e0[:, col] / den[:, col]) * ob0_ref[hh].astype(F32)
                   + (e1[:, col] / den[:, col]) * ob1_ref[hh].astype(F32)
                   + (e2[:, col] / den[:, col]) * ob2_ref[hh].astype(F32))
            ob_scr[:, hh * LANES:(hh + 1) * LANES] = mix.astype(BF16)

    h = h_scr[...]
    oa = jnp.concatenate([oa_ref[i] for i in range(NA_HEADS)], axis=-1)
    oc = jnp.concatenate([oc_ref[i] for i in range(DIFF_HEADS)], axis=-1)

    def gate(w_ref, b_ref):
        t = jnp.dot(h, w_ref[...], preferred_element_type=F32) + b_ref[...]
        return 1.0 / (1.0 + jnp.exp(-t))

    y = gate(wga_ref, bga_ref) * jnp.dot(oa, wa_ref[...], preferred_element_type=F32)
    y = y + gate(wgb_ref, bgb_ref) * jnp.dot(ob_scr[...], wb_ref[...], preferred_element_type=F32)
    y = y + gate(wgc_ref, bgc_ref) * jnp.dot(oc, wc_ref[...], preferred_element_type=F32)
    m_ref[...] = y.astype(BF16)


def _merge(x2, seq, g, sc, sh, oa, obs, lses, oc, wa, wb, wc, wg, bg):
    t, d = x2.shape
    tm = _pick(seq, 512)
    tn = _pick(d, 512)
    per = seq // tm
    nj = d // tn
    row = lambda i, j: (i, 0)
    heads = lambda nh: pl.BlockSpec((nh, tm, LANES), lambda i, j: (0, i, 0))
    wcol = lambda k: pl.BlockSpec((k, tn), lambda i, j: (0, j))
    gcol = lambda sec: pl.BlockSpec((d, tn), lambda i, j: (0, sec * nj + j))
    bcol = lambda sec: pl.BlockSpec((1, tn), lambda i, j: (0, sec * nj + j))
    modspec = pl.BlockSpec((None, 1, d), lambda i, j: (i // per, 0, 0))
    bg2 = bg.reshape(1, 3 * d)
    return pl.pallas_call(
        _merge_kernel,
        grid=(t // tm, nj),
        in_specs=[
            pl.BlockSpec((tm, d), row), pl.BlockSpec((1, d), lambda i, j: (0, 0)), modspec, modspec,
            heads(NA_HEADS), heads(DIL_HEADS), heads(DIL_HEADS), heads(DIL_HEADS),
            pl.BlockSpec((tm, LANES), row), pl.BlockSpec((tm, LANES), row), pl.BlockSpec((tm, LANES), row),
            heads(DIFF_HEADS),
            wcol(wa.shape[0]), wcol(wb.shape[0]), wcol(wc.shape[0]),
            gcol(0), gcol(1), gcol(2), bcol(0), bcol(1), bcol(2),
        ],
        out_specs=pl.BlockSpec((tm, tn), lambda i, j: (i, j)),
        out_shape=jax.ShapeDtypeStruct((t, d), BF16),
        scratch_shapes=[pltpu.VMEM((tm, d), BF16), pltpu.VMEM((tm, DIL_HEADS * LANES), BF16)],
        compiler_params=_cparams(("parallel", "arbitrary")),
    )(x2, g, sc, sh, oa, *obs, *lses, oc, wa, wb, wc, wg, wg, wg, bg2, bg2, bg2)


def _outproj_kernel(m_ref, w_ref, x_ref, g1_ref, o_ref):
    y = jnp.dot(m_ref[...], w_ref[...], preferred_element_type=F32)
    o_ref[...] = x_ref[...] + g1_ref[...] * y


def _outproj(m, w, x2, seq, g1):
    t, d = x2.shape
    tm = _pick(seq, 1024)
    tn = _pick(d, 1024)
    per = seq // tm
    return pl.pallas_call(
        _outproj_kernel,
        grid=(t // tm, d // tn),
        in_specs=[
            pl.BlockSpec((tm, d), lambda i, j: (i, 0)),
            pl.BlockSpec((d, tn), lambda i, j: (0, j)),
            pl.BlockSpec((tm, tn), lambda i, j: (i, j)),
            pl.BlockSpec((None, 1, tn), lambda i, j: (i // per, 0, j)),
        ],
        out_specs=pl.BlockSpec((tm, tn), lambda i, j: (i, j)),
        out_shape=jax.ShapeDtypeStruct((t, d), F32),
        compiler_params=_cparams(("parallel", "parallel")),
    )(m, w, x2, g1)


def _mlp_kernel(x_ref, g_ref, sc_ref, sh_ref, g2_ref, w1_ref, w2_ref, o_ref, h_scr):
    k = pl.program_id(1)

    @pl.when(k == 0)
    def _():
        _modnorm_to(h_scr, x_ref, g_ref, sc_ref, sh_ref)

    u = jnp.maximum(jnp.dot(h_scr[...], w1_ref[...], preferred_element_type=F32), 0.0)
    f = jnp.dot((u * u).astype(BF16), w2_ref[...], preferred_element_type=F32)

    @pl.when(k == 0)
    def _():
        o_ref[...] = f

    @pl.when(k > 0)
    def _():
        o_ref[...] += f

    @pl.when(k == pl.num_programs(1) - 1)
    def _():
        o_ref[...] = x_ref[...] + g2_ref[...] * o_ref[...]


def _mlp(x2, seq, g, sc, sh, g2, w1, w2):
    t, d = x2.shape
    ff = w1.shape[1]
    tm = _pick(seq, 1024)
    tf = _pick(ff, 512)
    per = seq // tm
    modspec = pl.BlockSpec((None, 1, d), lambda i, k: (i // per, 0, 0))
    return pl.pallas_call(
        _mlp_kernel,
        grid=(t // tm, ff // tf),
        in_specs=[
            pl.BlockSpec((tm, d), lambda i, k: (i, 0)),
            pl.BlockSpec((1, d), lambda i, k: (0, 0)),
            modspec, modspec, modspec,
            pl.BlockSpec((d, tf), lambda i, k: (0, k)),
            pl.BlockSpec((tf, d), lambda i, k: (k, 0)),
        ],
        out_specs=pl.BlockSpec((tm, d), lambda i, k: (i, 0)),
        out_shape=jax.ShapeDtypeStruct((t, d), F32),
        scratch_shapes=[pltpu.VMEM((tm, d), BF16)],
        compiler_params=_cparams(("parallel", "arbitrary")),
    )(x2, g, sc, sh, g2, w1, w2)


def _layer(x2, b, s, mods, layer_idx, p, tabs_b, tabs_c):
    d = x2.shape[1]
    sh1, sc1, g1, sh2, sc2, g2 = [m.reshape(b, 1, d) for m in jnp.split(mods, 6, axis=-1)]
    z = _inproj(x2, s, p["norm1_g"], sc1, sh1, p["w_in"])
    z4 = z.reshape(N_SLABS, b, s, LANES)
    oa = _na(z4, p["bias_a"], p["qn_a"], p["kn_a"]).reshape(NA_HEADS, b * s, LANES)
    obs, lses = zip(*[_dil(z4, g, tabs_b, p["qn_b"], p["kn_b"]) for g in range(len(DILATIONS))])
    lam_init = 0.8 - 0.6 * math.exp(-0.3 * layer_idx)
    oc = _diff(z4, tabs_c, p["qn_c"], p["kn_c"], p["lamv"], p["subln_c"], lam_init)
    oc = oc.reshape(DIFF_HEADS, b * s, LANES)
    m = _merge(x2, s, p["norm1_g"], sc1, sh1, oa, obs, lses, oc,
               p["w_br_a"], p["w_br_b"], p["w_br_c"], p["w_gate"], p["b_gate"])
    x2 = _outproj(m, p["w_out"], x2, s, g1)
    return _mlp(x2, s, p["norm2_g"], sc2, sh2, g2, p["w_ff1"], p["w_ff2"])


def kernel(x_prompt, x_sample, c_prompt, c_sample, norm1_g, w_ada, b_ada, w_in, qn_a, kn_a, rpb_a, qn_b, kn_b, qn_c, kn_c, lam_q1, lam_k1, lam_q2, lam_k2, subln_c, w_br_a, w_br_b, w_br_c, w_gate, b_gate, w_out, norm2_g, w_ff1, w_ff2):
    depth = w_in.shape[0]
    bp, sp, d = x_prompt.shape
    bs, ss, _ = x_sample.shape
    mods = _ada(jnp.concatenate([c_prompt, c_sample], axis=0), w_ada, b_ada)

    cat2 = lambda v: jnp.concatenate([v, v], axis=-1)
    tabs = {s: (_rope_tables(s, DIL_HALF, (0,)), _rope_tables(s, DIFF_HALF, (0, DIFF_QK_DIM)))
            for s in {sp, ss}}
    xp = x_prompt.reshape(bp * sp, d)
    xs = x_sample.reshape(bs * ss, d)
    for l in range(depth):
        p = {
            "norm1_g": norm1_g[l].reshape(1, d), "norm2_g": norm2_g[l].reshape(1, d),
            "w_in": w_in[l].astype(BF16), "w_gate": w_gate[l].astype(BF16), "b_gate": b_gate[l],
            "w_br_a": w_br_a[l].astype(BF16), "w_br_b": w_br_b[l].astype(BF16),
            "w_br_c": w_br_c[l].astype(BF16), "w_out": w_out[l].astype(BF16),
            "w_ff1": w_ff1[l].astype(BF16), "w_ff2": w_ff2[l].astype(BF16),
            "qn_a": qn_a[l].reshape(1, LANES), "kn_a": kn_a[l].reshape(1, LANES),
            "qn_b": qn_b[l].reshape(1, LANES), "kn_b": kn_b[l].reshape(1, LANES),
            "qn_c": cat2(qn_c[l]).reshape(1, LANES), "kn_c": cat2(kn_c[l]).reshape(1, LANES),
            "lamv": jnp.stack([lam_q1[l], lam_k1[l], lam_q2[l], lam_k2[l]]).astype(F32),
            "subln_c": subln_c[l].reshape(1, LANES),
            "bias_a": _na_bias(rpb_a[l]),
        }
        xp = _layer(xp, bp, sp, mods[l, :bp], l, p, *tabs[sp])
        xs = _layer(xs, bs, ss, mods[l, bp:], l, p, *tabs[ss])
    return xp.reshape(bp, sp, d), xs.reshape(bs, ss, d)
```

```python
import functools
import math

import jax
import jax.numpy as jnp
from jax import lax
from jax.experimental import pallas as pl
from jax.experimental.pallas import tpu as pltpu

F32 = jnp.float32
BF16 = jnp.bfloat16

LANES = 128
HEAD_DIM = 128
ROPE_THETA = 500000.0
EPS = 1e-6
NEG = -1e30
LOG2E = 1.4426950408889634
GRID_W = 64
WIN_ROWS = 8
WIN_COLS = 16
NA_HEADS = 4
DILATIONS = (1, 4, 16)
DIL_HEADS = 2
DIL_SPAN = 64
DIFF_HEADS = 6
DIFF_QK_DIM = 64
QA, KA, VA, B0, QC, KC, VC = 0, 4, 8, 12, 18, 24, 30

VMEM_LIMIT = 56 * 1024 * 1024


def _split_w_in(w):
    cols = lambda a, b: w[:, a * LANES:b * LANES]
    grp = lambda g: [cols(12 + 2 * g, 14 + 2 * g), cols(18 + 2 * g, 20 + 2 * g), cols(24 + 2 * g, 26 + 2 * g)]
    main = jnp.concatenate([cols(0, 12)] + grp(0) + [cols(30, 48)], axis=1)
    dil = jnp.concatenate(grp(1) + grp(2), axis=1)
    return main, dil


def _cparams(sem, vmem=VMEM_LIMIT):
    return pltpu.CompilerParams(dimension_semantics=sem, vmem_limit_bytes=vmem)


def _pick(n, pref):
    t = min(n, pref)
    while n % t:
        t //= 2
    return t


def _rms(x, width):
    return x * lax.rsqrt(jnp.sum(x * x, axis=-1, keepdims=True) * (1.0 / width) + EPS)


def _ada_kernel(c_ref, w_ref, b_ref, o_ref):
    c = c_ref[...]
    a = c * (1.0 / (1.0 + jnp.exp(-c)))
    o_ref[0] = jnp.dot(a.astype(BF16), w_ref[0].astype(BF16), preferred_element_type=F32) + b_ref[0]


def _ada(c_all, w_ada, b_ada):
    depth, d, n = w_ada.shape
    nb = c_all.shape[0]
    tn = _pick(n, 1024)
    return pl.pallas_call(
        _ada_kernel,
        grid=(depth, n // tn),
        in_specs=[
            pl.BlockSpec((nb, d), lambda l, j: (0, 0)),
            pl.BlockSpec((1, d, tn), lambda l, j: (l, 0, j)),
            pl.BlockSpec((1, 1, tn), lambda l, j: (l, 0, j)),
        ],
        out_specs=pl.BlockSpec((1, nb, tn), lambda l, j: (l, 0, j)),
        out_shape=jax.ShapeDtypeStruct((depth, nb, n), F32),
        compiler_params=_cparams(("parallel", "parallel")),
        name="ada_mod",
    )(c_all, w_ada, b_ada.reshape(depth, 1, n))


NORM_ROWS = 64


def _modnorm_to(h_scr, x_ref, g_ref, sc_ref, sh_ref):
    tm, d = x_ref.shape
    gs = g_ref[...] * (1.0 + sc_ref[...])
    sh = sh_ref[...]

    def body(i, carry):
        sl = pl.ds(pl.multiple_of(i * NORM_ROWS, NORM_ROWS), NORM_ROWS)
        x = x_ref[sl, :]
        h_scr[sl, :] = (_rms(x, d) * gs + sh).astype(BF16)
        return carry

    lax.fori_loop(0, tm // NORM_ROWS, body, 0, unroll=2)


def _inproj_kernel(x_ref, g_ref, sc_ref, sh_ref, w_ref, z_ref, h_scr):
    @pl.when(pl.program_id(1) == 0)
    def _():
        _modnorm_to(h_scr, x_ref, g_ref, sc_ref, sh_ref)

    acc = jnp.dot(h_scr[...], w_ref[...], preferred_element_type=F32)
    for c in range(z_ref.shape[0]):
        z_ref[c] = acc[:, c * LANES:(c + 1) * LANES].astype(z_ref.dtype)


def _inproj(x2, seq, g, sc, sh, w, out_dtype, name):
    t, d = x2.shape
    n = w.shape[1]
    tm = _pick(seq, 1024)
    tn = _pick(n, 768)
    per = seq // tm
    return pl.pallas_call(
        _inproj_kernel,
        grid=(t // tm, n // tn),
        in_specs=[
            pl.BlockSpec((tm, d), lambda i, j: (i, 0)),
            pl.BlockSpec((1, d), lambda i, j: (0, 0)),
            pl.BlockSpec((None, 1, d), lambda i, j: (i // per, 0, 0)),
            pl.BlockSpec((None, 1, d), lambda i, j: (i // per, 0, 0)),
            pl.BlockSpec((d, tn), lambda i, j: (0, j)),
        ],
        out_specs=pl.BlockSpec((tn // LANES, tm, LANES), lambda i, j: (j, i, 0)),
        out_shape=jax.ShapeDtypeStruct((n // LANES, t, LANES), out_dtype),
        scratch_shapes=[pltpu.VMEM((tm, d), BF16)],
        compiler_params=_cparams(("parallel", "arbitrary")),
        name=name,
    )(x2, g, sc, sh, w)


def _na_bias(rpb):
    c = jnp.arange(GRID_W)
    cs = jnp.clip(c - WIN_COLS // 2, 0, GRID_W - WIN_COLS)
    kc = jnp.arange(GRID_W)
    inwin = (kc[None, :] >= cs[:, None]) & (kc[None, :] < cs[:, None] + WIN_COLS)
    dc = kc[None, :] - c[:, None] + WIN_COLS - 1
    oh_c = (dc[:, :, None] == jnp.arange(2 * WIN_COLS - 1)).astype(F32)
    dr = jnp.arange(WIN_ROWS)[None, :] - jnp.arange(WIN_ROWS)[:, None] + WIN_ROWS - 1
    oh_r = (dr[:, :, None] == jnp.arange(2 * WIN_ROWS - 1)).astype(F32)
    b = jnp.einsum("hij,dri,ckj->hdcrk", rpb.astype(F32), oh_r, oh_c, precision=lax.Precision.HIGHEST)
    b = jnp.where(inwin[None, None, :, None, :], b, NEG)
    return b.reshape(rpb.shape[0], WIN_ROWS, GRID_W, WIN_ROWS * GRID_W)


NA_GROUP = 8


def _na_kernel(q_ref, k_ref, v_ref, bias_ref, qn_ref, kn_ref, o_ref, qn_scr, kn_scr, *, rows):
    band = WIN_ROWS * GRID_W
    kg = kn_ref[...]
    qg = qn_ref[...] * HEAD_DIM ** -0.5

    def norm(i, carry):
        sl = pl.ds(pl.multiple_of(i * 256, 256), 256)
        kn_scr[sl, :] = (_rms(k_ref[sl, :].astype(F32), HEAD_DIM) * kg).astype(BF16)
        qn_scr[sl, :] = (_rms(q_ref[sl, :].astype(F32), HEAD_DIM) * qg).astype(BF16)
        return carry

    lax.fori_loop(0, rows * GRID_W // 256, norm, 0)

    def body(g, carry):
        rr = [g * NA_GROUP + i for i in range(NA_GROUP)]
        rs = [jnp.clip(r - WIN_ROWS // 2, 0, rows - WIN_ROWS) for r in rr]
        qsl = [pl.ds(pl.multiple_of(r * GRID_W, GRID_W), GRID_W) for r in rr]
        ksl = [pl.ds(pl.multiple_of(r0 * GRID_W, GRID_W), band) for r0 in rs]
        s = [lax.dot_general(qn_scr[qsl[i], :], kn_scr[ksl[i], :], (((1,), (1,)), ((), ())),
                             preferred_element_type=F32) for i in range(NA_GROUP)]
        p, l = [], []
        for i in range(NA_GROUP):
            si = s[i] + bias_ref[rr[i] - rs[i]]
            pi = jnp.exp(si - jnp.max(si, axis=-1, keepdims=True))
            l.append(jnp.sum(pi, axis=-1, keepdims=True))
            p.append(pi.astype(BF16))
        for i in range(NA_GROUP):
            o = jnp.dot(p[i], v_ref[ksl[i], :], preferred_element_type=F32) / l[i]
            o_ref[qsl[i], :] = o.astype(BF16)
        return carry

    lax.fori_loop(0, rows // NA_GROUP, body, 0)


def _na(z4, bias, qn, kn):
    _, b, s, _ = z4.shape
    rows = s // GRID_W
    assert rows >= WIN_ROWS and rows % NA_GROUP == 0 and s % 256 == 0
    slab = lambda off: pl.BlockSpec((None, None, s, LANES), lambda bi, h: (off + h, bi, 0, 0))
    return pl.pallas_call(
        functools.partial(_na_kernel, rows=rows),
        grid=(b, NA_HEADS),
        in_specs=[
            slab(QA), slab(KA), slab(VA),
            pl.BlockSpec((None, WIN_ROWS, GRID_W, WIN_ROWS * GRID_W), lambda bi, h: (h, 0, 0, 0)),
            pl.BlockSpec((1, LANES), lambda bi, h: (0, 0)),
            pl.BlockSpec((1, LANES), lambda bi, h: (0, 0)),
        ],
        out_specs=pl.BlockSpec((None, None, s, LANES), lambda bi, h: (h, bi, 0, 0)),
        out_shape=jax.ShapeDtypeStruct((NA_HEADS, b, s, LANES), BF16),
        scratch_shapes=[pltpu.VMEM((s, LANES), BF16), pltpu.VMEM((s, LANES), BF16)],
        compiler_params=_cparams(("parallel", "parallel")),
        name="mixer_a",
    )(z4, z4, z4, bias, qn, kn)


def _rope_tables(seq, half, starts):
    inv = ROPE_THETA ** (-jnp.arange(half, dtype=F32) / half)
    ang = jnp.arange(seq, dtype=F32)[:, None] * inv[None, :]
    cos, sin = jnp.cos(ang), jnp.sin(ang)
    c = jnp.ones((seq, LANES), F32)
    sn = jnp.zeros((seq, LANES), F32)
    for st in starts:
        c = c.at[:, st:st + half].set(cos).at[:, st + half:st + 2 * half].set(cos)
        sn = sn.at[:, st:st + half].set(-sin).at[:, st + half:st + 2 * half].set(sin)
    return c, sn


def _rope(x, c, sn, half, starts):
    lane = lax.broadcasted_iota(jnp.int32, x.shape, 1)
    first = functools.reduce(jnp.logical_or, [(lane >= st) & (lane < st + half) for st in starts])
    rot = jnp.where(first, pltpu.roll(x, LANES - half, 1), pltpu.roll(x, half, 1))
    return x * c + rot * sn


DIL_HALF = HEAD_DIM // 4 // 2
DIL_GROUP = 4


def _dil_kernel(q0_ref, k0_ref, v0_ref, q1_ref, k1_ref, v1_ref, q2_ref, k2_ref, v2_ref,
                c_ref, sn_ref, qn_ref, kn_ref, o_ref,
                qn_scr, kn_scr, v_scr, og0, og1, og2, ls0, ls1, ls2, *, seq):
    kg = kn_ref[...]
    qg = qn_ref[...] * HEAD_DIM ** -0.5
    groups = ((q0_ref, k0_ref, v0_ref, og0, ls0), (q1_ref, k1_ref, v1_ref, og1, ls1),
              (q2_ref, k2_ref, v2_ref, og2, ls2))

    for (qr, kr, vr, og, ls), d in zip(groups, DILATIONS):
        n = seq // d
        tq = min(128, n)
        kw = min(2 * tq, n)
        ck = min(256, n)
        nb = n // tq
        nc = n // ck

        def cls(start, size, d=d):
            return pl.ds(start, size) if d == 1 else pl.ds(start, size, stride=d)

        def prep(u, carry, d=d, ck=ck, nc=nc, qr=qr, kr=kr, vr=vr, cls=cls):
            start = u // nc + d * ck * (u % nc)
            if d == 1:
                start = pl.multiple_of(start, ck)
            src = cls(start, ck)
            dst = pl.ds(pl.multiple_of(u * ck, ck), ck)
            c, sn = c_ref[src, :], sn_ref[src, :]
            k = _rms(kr[src, :].astype(F32), HEAD_DIM) * kg
            kn_scr[dst, :] = _rope(k, c, sn, DIL_HALF, (0,)).astype(BF16)
            q = _rms(qr[src, :].astype(F32), HEAD_DIM) * qg
            qn_scr[dst, :] = _rope(q, c, sn, DIL_HALF, (0,)).astype(BF16)
            v_scr[dst, :] = vr[src, :].astype(BF16)
            return carry

        lax.fori_loop(0, seq // ck, prep, 0, unroll=2)

        def qblocks(gi, carry, d=d, n=n, tq=tq, kw=kw, nb=nb, og=og, ls=ls, cls=cls):
            uu = [gi * DIL_GROUP + i for i in range(DIL_GROUP)]
            q0 = [(u % nb) * tq for u in uu]
            ks = [jnp.clip(q - DIL_SPAN, 0, n - kw) for q in q0]
            ksl = [pl.ds(pl.multiple_of((u // nb) * n + k0, DIL_SPAN), kw) for u, k0 in zip(uu, ks)]
            s = [lax.dot_general(qn_scr[pl.ds(pl.multiple_of(u * tq, tq), tq), :], kn_scr[ksl[i], :],
                                 (((1,), (1,)), ((), ())), preferred_element_type=F32)
                 for i, u in enumerate(uu)]
            off = (lax.broadcasted_iota(jnp.int32, (tq, kw), 1) - lax.broadcasted_iota(jnp.int32, (tq, kw), 0))
            p, l, lse = [], [], []
            for i in range(DIL_GROUP):
                si = jnp.where(jnp.abs(off + (ks[i] - q0[i])) <= DIL_SPAN, s[i], NEG)
                m = jnp.max(si, axis=-1, keepdims=True)
                pi = jnp.exp(si - m)
                li = jnp.sum(pi, axis=-1, keepdims=True)
                p.append(pi.astype(BF16))
                l.append(li)
                lse.append(m + jnp.log(li))
            for i, u in enumerate(uu):
                start = u // nb + d * q0[i]
                if d == 1:
                    start = pl.multiple_of(start, tq)
                o = jnp.dot(p[i], v_scr[ksl[i], :], preferred_element_type=F32) / l[i]
                og[cls(start, tq), :] = o
                ls[cls(start, tq), :] = jnp.broadcast_to(lse[i], (tq, LANES))
            return carry

        lax.fori_loop(0, seq // tq // DIL_GROUP, qblocks, 0)

    def mix(i, carry):
        sl = pl.ds(pl.multiple_of(i * 256, 256), 256)
        l0, l1, l2 = ls0[sl, :], ls1[sl, :], ls2[sl, :]
        mx = jnp.maximum(jnp.maximum(l0, l1), l2)
        e0, e1, e2 = jnp.exp(l0 - mx), jnp.exp(l1 - mx), jnp.exp(l2 - mx)
        o = (e0 * og0[sl, :] + e1 * og1[sl, :] + e2 * og2[sl, :]) / (e0 + e1 + e2)
        o_ref[sl, :] = o.astype(BF16)
        return carry

    lax.fori_loop(0, seq // 256, mix, 0)


def _dil(zm4, zd4, tabs, qn, kn):
    _, b, s, _ = zm4.shape
    assert s % 256 == 0 and s // DILATIONS[-1] >= DIL_SPAN and (s // 128) % DIL_GROUP == 0
    once = dict(pipeline_mode=pl.Buffered(1))
    main = lambda off: pl.BlockSpec((None, None, s, LANES), lambda bi, hh: (off + hh, bi, 0, 0))
    dil = lambda off: pl.BlockSpec((None, None, s, LANES), lambda bi, hh: (off + hh, bi, 0, 0), **once)
    tab = pl.BlockSpec((s, LANES), lambda bi, hh: (0, 0), **once)
    vec = pl.BlockSpec((1, LANES), lambda bi, hh: (0, 0))
    return pl.pallas_call(
        functools.partial(_dil_kernel, seq=s),
        grid=(b, DIL_HEADS),
        in_specs=[main(B0), main(B0 + 2), main(B0 + 4),
                  dil(0), dil(2), dil(4), dil(6), dil(8), dil(10),
                  tab, tab, vec, vec],
        out_specs=pl.BlockSpec((None, None, s, LANES), lambda bi, hh: (hh, bi, 0, 0)),
        out_shape=jax.ShapeDtypeStruct((DIL_HEADS, b, s, LANES), BF16),
        scratch_shapes=[pltpu.VMEM((s, LANES), BF16)] * 3 + [pltpu.VMEM((s, LANES), F32)] * 6,
        compiler_params=_cparams(("parallel", "parallel")),
        name="mixer_b",
    )(zm4, zm4, zm4, zd4, zd4, zd4, zd4, zd4, zd4, *tabs, qn, kn)


DIFF_HALF = DIFF_QK_DIM // 4 // 2
DIFF_TQ = 256
DIFF_TK = 512


def _rms_halves(x):
    lo = lax.broadcasted_iota(jnp.int32, x.shape, 1) < DIFF_QK_DIM
    xx = x * x
    s_lo = jnp.sum(jnp.where(lo, xx, 0.0), axis=-1, keepdims=True)
    s_hi = jnp.sum(jnp.where(lo, 0.0, xx), axis=-1, keepdims=True)
    r = lax.rsqrt(jnp.where(lo, s_lo, s_hi) * (1.0 / DIFF_QK_DIM) + EPS)
    return x * r


def _diff_kernel(q_ref, k_ref, v_ref, c_ref, sn_ref, qn_ref, kn_ref, lam_ref, sub_ref, o_ref, kt_scr, q2_scr,
                 *, seq, tq, tk, lam_init):
    starts = (0, DIFF_QK_DIM)

    @pl.when(pl.program_id(2) == 0)
    def _():
        kg = kn_ref[...]
        qg = qn_ref[...] * (DIFF_QK_DIM ** -0.5 * LOG2E)

        def knorm(i, carry):
            sl = pl.ds(pl.multiple_of(i * tk, tk), tk)
            k = _rms_halves(k_ref[sl, :].astype(F32)) * kg
            k = _rope(k, c_ref[sl, :], sn_ref[sl, :], DIFF_HALF, starts)
            kt_scr[i] = k.T.astype(BF16)
            return carry

        lax.fori_loop(0, seq // tk, knorm, 0, unroll=2)

        def qnorm(i, carry):
            sl = pl.ds(pl.multiple_of(i * tq, tq), tq)
            q = _rms_halves(q_ref[sl, :].astype(F32)) * qg
            q = _rope(q, c_ref[sl, :], sn_ref[sl, :], DIFF_HALF, starts)
            lo = lax.broadcasted_iota(jnp.int32, q.shape, 1) < DIFF_QK_DIM
            q2_scr[i, 0:tq, :] = jnp.where(lo, q, 0.0).astype(BF16)
            q2_scr[i, tq:2 * tq, :] = jnp.where(lo, 0.0, q).astype(BF16)
            return carry

        lax.fori_loop(0, seq // tq, qnorm, 0, unroll=2)

    lv = lam_ref[...]
    lam = (jnp.exp(jnp.sum(lv[0:1] * lv[1:2], axis=-1, keepdims=True))
           - jnp.exp(jnp.sum(lv[2:3] * lv[3:4], axis=-1, keepdims=True)) + lam_init)

    q2 = q2_scr[pl.program_id(2)]
    m = jnp.full((2 * tq, 1), -jnp.inf, F32)
    l = jnp.zeros((2 * tq, LANES), F32)
    acc = jnp.zeros((2 * tq, HEAD_DIM), F32)
    nk = seq // tk
    s_next = jnp.dot(q2, kt_scr[0], preferred_element_type=F32)
    for j in range(nk):
        s = s_next
        if j + 1 < nk:
            s_next = jnp.dot(q2, kt_scr[j + 1], preferred_element_type=F32)
        m_new = jnp.maximum(m, jnp.max(s, axis=-1, keepdims=True))
        alpha = jnp.exp2(m - m_new)
        p = jnp.exp2(s - m_new)
        l = alpha * l + functools.reduce(jnp.add, [p[:, c * LANES:(c + 1) * LANES] for c in range(tk // LANES)])
        acc = alpha * acc + jnp.dot(p.astype(BF16), v_ref[j * tk:(j + 1) * tk, :], preferred_element_type=F32)
        m = m_new
    on = acc / jnp.sum(l, axis=-1, keepdims=True)
    o = on[:tq] - lam * on[tq:]
    o_ref[...] = (_rms(o, HEAD_DIM) * sub_ref[...] * (1.0 - lam_init)).astype(BF16)


def _diff(z4, tabs, qn, kn, lamv, sub, lam_init):
    _, b, s, _ = z4.shape
    tq = min(DIFF_TQ, s)
    tk = min(DIFF_TK, s)
    assert s % tq == 0 and s % tk == 0
    kv = lambda off: pl.BlockSpec((None, None, s, LANES), lambda bi, h, qi: (off + h, bi, 0, 0))
    tab = pl.BlockSpec((s, LANES), lambda bi, h, qi: (0, 0))
    vec = pl.BlockSpec((1, LANES), lambda bi, h, qi: (0, 0))
    return pl.pallas_call(
        functools.partial(_diff_kernel, seq=s, tq=tq, tk=tk, lam_init=lam_init),
        grid=(b, DIFF_HEADS, s // tq),
        in_specs=[
            kv(QC), kv(KC), kv(VC), tab, tab, vec, vec,
            pl.BlockSpec((4, DIFF_QK_DIM), lambda bi, h, qi: (0, 0)),
            vec,
        ],
        out_specs=pl.BlockSpec((None, None, tq, LANES), lambda bi, h, qi: (h, bi, qi, 0)),
        out_shape=jax.ShapeDtypeStruct((DIFF_HEADS, b, s, LANES), BF16),
        scratch_shapes=[pltpu.VMEM((s // tk, LANES, tk), BF16), pltpu.VMEM((s // tq, 2 * tq, LANES), BF16)],
        compiler_params=_cparams(("parallel", "parallel", "arbitrary")),
        name="mixer_c",
    )(z4, z4, z4, *tabs, qn, kn, lamv, sub)


def _merge_kernel(x_ref, g_ref, sc_ref, sh_ref, oa_ref, ob_ref, oc_ref, wa_ref, wb_ref, wc_ref,
                  wga_ref, wgb_ref, wgc_ref, bga_ref, bgb_ref, bgc_ref, m_ref, h_scr):
    @pl.when(pl.program_id(1) == 0)
    def _():
        _modnorm_to(h_scr, x_ref, g_ref, sc_ref, sh_ref)

    h = h_scr[...]
    cat = lambda ref: jnp.concatenate([ref[i] for i in range(ref.shape[0])], axis=-1)

    def gate(w_ref, b_ref):
        t = jnp.dot(h, w_ref[...], preferred_element_type=F32) + b_ref[...]
        return 1.0 / (1.0 + jnp.exp(-t))

    y = gate(wga_ref, bga_ref) * jnp.dot(cat(oa_ref), wa_ref[...], preferred_element_type=F32)
    y = y + gate(wgb_ref, bgb_ref) * jnp.dot(cat(ob_ref), wb_ref[...], preferred_element_type=F32)
    y = y + gate(wgc_ref, bgc_ref) * jnp.dot(cat(oc_ref), wc_ref[...], preferred_element_type=F32)
    m_ref[...] = y.astype(BF16)


def _merge(x2, seq, g, sc, sh, oa, ob, oc, wa, wb, wc, wg, bg):
    t, d = x2.shape
    tm = _pick(seq, 512)
    tn = _pick(d, 512)
    per = seq // tm
    nj = d // tn
    heads = lambda nh: pl.BlockSpec((nh, tm, LANES), lambda i, j: (0, i, 0))
    wcol = lambda k: pl.BlockSpec((k, tn), lambda i, j: (0, j))
    gcol = lambda sec: pl.BlockSpec((d, tn), lambda i, j: (0, sec * nj + j))
    bcol = lambda sec: pl.BlockSpec((1, tn), lambda i, j: (0, sec * nj + j))
    modspec = pl.BlockSpec((None, 1, d), lambda i, j: (i // per, 0, 0))
    bg2 = bg.reshape(1, 3 * d)
    return pl.pallas_call(
        _merge_kernel,
        grid=(t // tm, nj),
        in_specs=[
            pl.BlockSpec((tm, d), lambda i, j: (i, 0)), pl.BlockSpec((1, d), lambda i, j: (0, 0)),
            modspec, modspec,
            heads(NA_HEADS), heads(DIL_HEADS), heads(DIFF_HEADS),
            wcol(wa.shape[0]), wcol(wb.shape[0]), wcol(wc.shape[0]),
            gcol(0), gcol(1), gcol(2), bcol(0), bcol(1), bcol(2),
        ],
        out_specs=pl.BlockSpec((tm, tn), lambda i, j: (i, j)),
        out_shape=jax.ShapeDtypeStruct((t, d), BF16),
        scratch_shapes=[pltpu.VMEM((tm, d), BF16)],
        compiler_params=_cparams(("parallel", "arbitrary")),
        name="merge",
    )(x2, g, sc, sh, oa, ob, oc, wa, wb, wc, wg, wg, wg, bg2, bg2, bg2)


def _outproj_kernel(m_ref, w_ref, x_ref, g1_ref, o_ref):
    y = jnp.dot(m_ref[...], w_ref[...], preferred_element_type=F32)
    o_ref[...] = x_ref[...] + g1_ref[...] * y


def _outproj(m, w, x2, seq, g1):
    t, d = x2.shape
    tm = _pick(seq, 1024)
    tn = _pick(d, 1024)
    per = seq // tm
    return pl.pallas_call(
        _outproj_kernel,
        grid=(t // tm, d // tn),
        in_specs=[
            pl.BlockSpec((tm, d), lambda i, j: (i, 0)),
            pl.BlockSpec((d, tn), lambda i, j: (0, j)),
            pl.BlockSpec((tm, tn), lambda i, j: (i, j)),
            pl.BlockSpec((None, 1, tn), lambda i, j: (i // per, 0, j)),
        ],
        out_specs=pl.BlockSpec((tm, tn), lambda i, j: (i, j)),
        out_shape=jax.ShapeDtypeStruct((t, d), F32),
        compiler_params=_cparams(("parallel", "parallel")),
        name="out_proj",
    )(m, w, x2, g1)


MLP_ACC_COLS = 512


def _mlp_kernel(x_ref, g_ref, sc_ref, sh_ref, g2_ref, w1_ref, w2_ref, o_ref, h_scr):
    k = pl.program_id(1)

    @pl.when(k == 0)
    def _():
        _modnorm_to(h_scr, x_ref, g_ref, sc_ref, sh_ref)

        o_ref[...] = jnp.zeros_like(o_ref)

    u = jnp.maximum(jnp.dot(h_scr[...], w1_ref[...], preferred_element_type=F32), 0.0)
    u = (u * u).astype(BF16)
    d = o_ref.shape[1]
    nc = _pick(d, MLP_ACC_COLS)
    for c in range(d // nc):
        cs = slice(c * nc, (c + 1) * nc)
        o_ref[:, cs] += jnp.dot(u, w2_ref[:, cs], preferred_element_type=F32)

    @pl.when(k == pl.num_programs(1) - 1)
    def _():
        o_ref[...] = x_ref[...] + g2_ref[...] * o_ref[...]


def _mlp(x2, seq, g, sc, sh, g2, w1, w2):
    t, d = x2.shape
    ff = w1.shape[1]
    tm = _pick(seq, 1024)
    tf = _pick(ff, 512)
    per = seq // tm
    modspec = pl.BlockSpec((None, 1, d), lambda i, k: (i // per, 0, 0))
    return pl.pallas_call(
        _mlp_kernel,
        grid=(t // tm, ff // tf),
        in_specs=[
            pl.BlockSpec((tm, d), lambda i, k: (i, 0)),
            pl.BlockSpec((1, d), lambda i, k: (0, 0)),
            modspec, modspec, modspec,
            pl.BlockSpec((d, tf), lambda i, k: (0, k)),
            pl.BlockSpec((tf, d), lambda i, k: (k, 0)),
        ],
        out_specs=pl.BlockSpec((tm, d), lambda i, k: (i, 0)),
        out_shape=jax.ShapeDtypeStruct((t, d), F32),
        scratch_shapes=[pltpu.VMEM((tm, d), BF16)],
        compiler_params=_cparams(("parallel", "arbitrary")),
        name="mlp",
    )(x2, g, sc, sh, g2, w1, w2)


def _layer(x2, b, s, mods, layer_idx, p, tabs_b, tabs_c):
    d = x2.shape[1]
    sh1, sc1, g1, sh2, sc2, g2 = [m.reshape(b, 1, d) for m in jnp.split(mods, 6, axis=-1)]
    zm = _inproj(x2, s, p["norm1_g"], sc1, sh1, p["w_in_main"], BF16, "in_proj_main")
    zd = _inproj(x2, s, p["norm1_g"], sc1, sh1, p["w_in_dil"], F32, "in_proj_dil")
    zm4 = zm.reshape(zm.shape[0], b, s, LANES)
    zd4 = zd.reshape(zd.shape[0], b, s, LANES)
    oa = _na(zm4, p["bias_a"], p["qn_a"], p["kn_a"]).reshape(NA_HEADS, b * s, LANES)
    ob = _dil(zm4, zd4, tabs_b, p["qn_b"], p["kn_b"]).reshape(DIL_HEADS, b * s, LANES)
    lam_init = 0.8 - 0.6 * math.exp(-0.3 * layer_idx)
    oc = _diff(zm4, tabs_c, p["qn_c"], p["kn_c"], p["lamv"], p["subln_c"], lam_init)
    oc = oc.reshape(DIFF_HEADS, b * s, LANES)
    m = _merge(x2, s, p["norm1_g"], sc1, sh1, oa, ob, oc,
               p["w_br_a"], p["w_br_b"], p["w_br_c"], p["w_gate"], p["b_gate"])
    x2 = _outproj(m, p["w_out"], x2, s, g1)
    return _mlp(x2, s, p["norm2_g"], sc2, sh2, g2, p["w_ff1"], p["w_ff2"])


def kernel(x_prompt, x_sample, c_prompt, c_sample, norm1_g, w_ada, b_ada, w_in, qn_a, kn_a, rpb_a, qn_b, kn_b, qn_c, kn_c, lam_q1, lam_k1, lam_q2, lam_k2, subln_c, w_br_a, w_br_b, w_br_c, w_gate, b_gate, w_out, norm2_g, w_ff1, w_ff2):
    depth = w_in.shape[0]
    bp, sp, d = x_prompt.shape
    bs, ss, _ = x_sample.shape
    mods = _ada(jnp.concatenate([c_prompt, c_sample], axis=0), w_ada, b_ada)

    cat2 = lambda v: jnp.concatenate([v, v], axis=-1)
    tabs = {s: (_rope_tables(s, DIL_HALF, (0,)), _rope_tables(s, DIFF_HALF, (0, DIFF_QK_DIM)))
            for s in {sp, ss}}
    xp = x_prompt.reshape(bp * sp, d)
    xs = x_sample.reshape(bs * ss, d)
    for l in range(depth):
        w_in_main, w_in_dil = _split_w_in(w_in[l].astype(BF16))
        p = {
            "norm1_g": norm1_g[l].reshape(1, d), "norm2_g": norm2_g[l].reshape(1, d),
            "w_in_main": w_in_main, "w_in_dil": w_in_dil,
            "w_gate": w_gate[l].astype(BF16), "b_gate": b_gate[l],
            "w_br_a": w_br_a[l].astype(BF16), "w_br_b": w_br_b[l].astype(BF16),
            "w_br_c": w_br_c[l].astype(BF16), "w_out": w_out[l].astype(BF16),
            "w_ff1": w_ff1[l].astype(BF16), "w_ff2": w_ff2[l].astype(BF16),
            "qn_a": qn_a[l].reshape(1, LANES), "kn_a": kn_a[l].reshape(1, LANES),
            "qn_b": qn_b[l].reshape(1, LANES), "kn_b": kn_b[l].reshape(1, LANES),
            "qn_c": cat2(qn_c[l]).reshape(1, LANES), "kn_c": cat2(kn_c[l]).reshape(1, LANES),
            "lamv": jnp.stack([lam_q1[l], lam_k1[l], lam_q2[l], lam_k2[l]]).astype(F32),
            "subln_c": subln_c[l].reshape(1, LANES),
            "bias_a": _na_bias(rpb_a[l]),
        }
        xp = _layer(xp, bp, sp, mods[l, :bp], l, p, *tabs[sp])
        xs = _layer(xs, bs, ss, mods[l, bp:], l, p, *tabs[ss])
    return xp.reshape(bp, sp, d), xs.reshape(bs, ss, d)
```

```python
import functools
import math

import jax
import jax.numpy as jnp
from jax import lax
from jax.experimental import pallas as pl
from jax.experimental.pallas import tpu as pltpu

F32 = jnp.float32
BF16 = jnp.bfloat16

LANES = 128
HEAD_DIM = 128
ROPE_THETA = 500000.0
EPS = 1e-6
NEG = -1e30
LOG2E = 1.4426950408889634
GRID_W = 64
WIN_ROWS = 8
WIN_COLS = 16
NA_HEADS = 4
DILATIONS = (1, 4, 16)
DIL_HEADS = 2
DIL_SPAN = 64
DIFF_HEADS = 6
DIFF_QK_DIM = 64
QA, KA, VA, B0, QC, KC, VC = 0, 4, 8, 12, 18, 24, 30

VMEM_LIMIT = 56 * 1024 * 1024


def _split_w_in(w):
    cols = lambda a, b: w[:, a * LANES:b * LANES]
    grp = lambda g: [cols(12 + 2 * g, 14 + 2 * g), cols(18 + 2 * g, 20 + 2 * g), cols(24 + 2 * g, 26 + 2 * g)]
    main = jnp.concatenate([cols(0, 12)] + grp(0) + [cols(30, 48)], axis=1)
    dil = jnp.concatenate(grp(1) + grp(2), axis=1)
    return main, dil


def _cparams(sem, vmem=VMEM_LIMIT):
    return pltpu.CompilerParams(dimension_semantics=sem, vmem_limit_bytes=vmem)


def _pick(n, pref):
    t = min(n, pref)
    while n % t:
        t //= 2
    return t


def _rms(x, width):
    return x * lax.rsqrt(jnp.sum(x * x, axis=-1, keepdims=True) * (1.0 / width) + EPS)


def _ada_kernel(c_ref, w_ref, b_ref, o_ref):
    c = c_ref[...]
    a = c * (1.0 / (1.0 + jnp.exp(-c)))
    o_ref[0] = jnp.dot(a.astype(BF16), w_ref[0].astype(BF16), preferred_element_type=F32) + b_ref[0]


def _ada(c_all, w_ada, b_ada):
    depth, d, n = w_ada.shape
    nb = c_all.shape[0]
    tn = _pick(n, 1024)
    return pl.pallas_call(
        _ada_kernel,
        grid=(depth, n // tn),
        in_specs=[
            pl.BlockSpec((nb, d), lambda l, j: (0, 0)),
            pl.BlockSpec((1, d, tn), lambda l, j: (l, 0, j)),
            pl.BlockSpec((1, 1, tn), lambda l, j: (l, 0, j)),
        ],
        out_specs=pl.BlockSpec((1, nb, tn), lambda l, j: (l, 0, j)),
        out_shape=jax.ShapeDtypeStruct((depth, nb, n), F32),
        compiler_params=_cparams(("parallel", "parallel")),
        name="ada_mod",
    )(c_all, w_ada, b_ada.reshape(depth, 1, n))


NORM_ROWS = 64


def _modnorm_to(h_scr, x_ref, g_ref, sc_ref, sh_ref):
    tm, d = x_ref.shape
    gs = g_ref[...] * (1.0 + sc_ref[...])
    sh = sh_ref[...]

    def body(i, carry):
        sl = pl.ds(pl.multiple_of(i * NORM_ROWS, NORM_ROWS), NORM_ROWS)
        x = x_ref[sl, :]
        h_scr[sl, :] = (_rms(x, d) * gs + sh).astype(BF16)
        return carry

    lax.fori_loop(0, tm // NORM_ROWS, body, 0, unroll=2)


def _inproj_kernel(x_ref, g_ref, sc_ref, sh_ref, w_ref, z_ref, h_scr):
    @pl.when(pl.program_id(1) == 0)
    def _():
        _modnorm_to(h_scr, x_ref, g_ref, sc_ref, sh_ref)

    acc = jnp.dot(h_scr[...], w_ref[...], preferred_element_type=F32)
    for c in range(z_ref.shape[0]):
        z_ref[c] = acc[:, c * LANES:(c + 1) * LANES].astype(z_ref.dtype)


def _inproj(x2, seq, g, sc, sh, w, out_dtype, name):
    t, d = x2.shape
    n = w.shape[1]
    tm = _pick(seq, 1024)
    tn = _pick(n, 768)
    per = seq // tm
    return pl.pallas_call(
        _inproj_kernel,
        grid=(t // tm, n // tn),
        in_specs=[
            pl.BlockSpec((tm, d), lambda i, j: (i, 0)),
            pl.BlockSpec((1, d), lambda i, j: (0, 0)),
            pl.BlockSpec((None, 1, d), lambda i, j: (i // per, 0, 0)),
            pl.BlockSpec((None, 1, d), lambda i, j: (i // per, 0, 0)),
            pl.BlockSpec((d, tn), lambda i, j: (0, j)),
        ],
        out_specs=pl.BlockSpec((tn // LANES, tm, LANES), lambda i, j: (j, i, 0)),
        out_shape=jax.ShapeDtypeStruct((n // LANES, t, LANES), out_dtype),
        scratch_shapes=[pltpu.VMEM((tm, d), BF16)],
        compiler_params=_cparams(("parallel", "arbitrary")),
        name=name,
    )(x2, g, sc, sh, w)


def _na_bias(rpb):
    c = jnp.arange(GRID_W)
    cs = jnp.clip(c - WIN_COLS // 2, 0, GRID_W - WIN_COLS)
    kc = jnp.arange(GRID_W)
    inwin = (kc[None, :] >= cs[:, None]) & (kc[None, :] < cs[:, None] + WIN_COLS)
    dc = kc[None, :] - c[:, None] + WIN_COLS - 1
    oh_c = (dc[:, :, None] == jnp.arange(2 * WIN_COLS - 1)).astype(F32)
    dr = jnp.arange(WIN_ROWS)[None, :] - jnp.arange(WIN_ROWS)[:, None] + WIN_ROWS - 1
    oh_r = (dr[:, :, None] == jnp.arange(2 * WIN_ROWS - 1)).astype(F32)
    b = jnp.einsum("hij,dri,ckj->hdcrk", rpb.astype(F32), oh_r, oh_c, precision=lax.Precision.HIGHEST)
    b = jnp.where(inwin[None, None, :, None, :], b, NEG)
    return b.reshape(rpb.shape[0], WIN_ROWS, GRID_W, WIN_ROWS * GRID_W)


NA_GROUP = 8


def _na_kernel(q_ref, k_ref, v_ref, bias_ref, qn_ref, kn_ref, o_ref, qn_scr, kn_scr, *, rows):
    band = WIN_ROWS * GRID_W
    kg = kn_ref[...]
    qg = qn_ref[...] * HEAD_DIM ** -0.5

    def norm(i, carry):
        sl = pl.ds(pl.multiple_of(i * 256, 256), 256)
        kn_scr[sl, :] = (_rms(k_ref[sl, :].astype(F32), HEAD_DIM) * kg).astype(BF16)
        qn_scr[sl, :] = (_rms(q_ref[sl, :].astype(F32), HEAD_DIM) * qg).astype(BF16)
        return carry

    lax.fori_loop(0, rows * GRID_W // 256, norm, 0)

    def body(g, carry):
        rr = [g * NA_GROUP + i for i in range(NA_GROUP)]
        rs = [jnp.clip(r - WIN_ROWS // 2, 0, rows - WIN_ROWS) for r in rr]
        qsl = [pl.ds(pl.multiple_of(r * GRID_W, GRID_W), GRID_W) for r in rr]
        ksl = [pl.ds(pl.multiple_of(r0 * GRID_W, GRID_W), band) for r0 in rs]
        s = [lax.dot_general(qn_scr[qsl[i], :], kn_scr[ksl[i], :], (((1,), (1,)), ((), ())),
                             preferred_element_type=F32) for i in range(NA_GROUP)]
        p, l = [], []
        for i in range(NA_GROUP):
            si = s[i] + bias_ref[rr[i] - rs[i]]
            pi = jnp.exp(si - jnp.max(si, axis=-1, keepdims=True))
            l.append(jnp.sum(pi, axis=-1, keepdims=True))
            p.append(pi.astype(BF16))
        for i in range(NA_GROUP):
            o = jnp.dot(p[i], v_ref[ksl[i], :], preferred_element_type=F32) / l[i]
            o_ref[qsl[i], :] = o.astype(BF16)
        return carry

    lax.fori_loop(0, rows // NA_GROUP, body, 0)


def _na(z4, bias, qn, kn):
    _, b, s, _ = z4.shape
    rows = s // GRID_W
    assert rows >= WIN_ROWS and rows % NA_GROUP == 0 and s % 256 == 0
    slab = lambda off: pl.BlockSpec((None, None, s, LANES), lambda bi, h: (off + h, bi, 0, 0))
    return pl.pallas_call(
        functools.partial(_na_kernel, rows=rows),
        grid=(b, NA_HEADS),
        in_specs=[
            slab(QA), slab(KA), slab(VA),
            pl.BlockSpec((None, WIN_ROWS, GRID_W, WIN_ROWS * GRID_W), lambda bi, h: (h, 0, 0, 0)),
            pl.BlockSpec((1, LANES), lambda bi, h: (0, 0)),
            pl.BlockSpec((1, LANES), lambda bi, h: (0, 0)),
        ],
        out_specs=pl.BlockSpec((None, None, s, LANES), lambda bi, h: (h, bi, 0, 0)),
        out_shape=jax.ShapeDtypeStruct((NA_HEADS, b, s, LANES), BF16),
        scratch_shapes=[pltpu.VMEM((s, LANES), BF16), pltpu.VMEM((s, LANES), BF16)],
        compiler_params=_cparams(("parallel", "parallel")),
        name="mixer_a",
    )(z4, z4, z4, bias, qn, kn)


def _rope_tables(seq, half, starts):
    inv = ROPE_THETA ** (-jnp.arange(half, dtype=F32) / half)
    ang = jnp.arange(seq, dtype=F32)[:, None] * inv[None, :]
    cos, sin = jnp.cos(ang), jnp.sin(ang)
    c = jnp.ones((seq, LANES), F32)
    sn = jnp.zeros((seq, LANES), F32)
    for st in starts:
        c = c.at[:, st:st + half].set(cos).at[:, st + half:st + 2 * half].set(cos)
        sn = sn.at[:, st:st + half].set(-sin).at[:, st + half:st + 2 * half].set(sin)
    return c, sn


def _rope(x, c, sn, half, starts):
    lane = lax.broadcasted_iota(jnp.int32, x.shape, 1)
    first = functools.reduce(jnp.logical_or, [(lane >= st) & (lane < st + half) for st in starts])
    rot = jnp.where(first, pltpu.roll(x, LANES - half, 1), pltpu.roll(x, half, 1))
    return x * c + rot * sn


DIL_HALF = HEAD_DIM // 4 // 2
DIL_GROUP = 4


def _dil_kernel(q0_ref, k0_ref, v0_ref, q1_ref, k1_ref, v1_ref, q2_ref, k2_ref, v2_ref,
                c_ref, sn_ref, qn_ref, kn_ref, o_ref,
                qn_scr, kn_scr, v_scr, og0, og1, og2, ls0, ls1, ls2, *, seq):
    kg = kn_ref[...]
    qg = qn_ref[...] * HEAD_DIM ** -0.5
    groups = ((q0_ref, k0_ref, v0_ref, og0, ls0), (q1_ref, k1_ref, v1_ref, og1, ls1),
              (q2_ref, k2_ref, v2_ref, og2, ls2))

    for (qr, kr, vr, og, ls), d in zip(groups, DILATIONS):
        n = seq // d
        tq = min(128, n)
        kw = min(2 * tq, n)
        ck = min(256, n)
        nb = n // tq
        nc = n // ck

        def cls(start, size, d=d):
            return pl.ds(start, size) if d == 1 else pl.ds(start, size, stride=d)

        def prep(u, carry, d=d, ck=ck, nc=nc, qr=qr, kr=kr, vr=vr, cls=cls):
            start = u // nc + d * ck * (u % nc)
            if d == 1:
                start = pl.multiple_of(start, ck)
            src = cls(start, ck)
            dst = pl.ds(pl.multiple_of(u * ck, ck), ck)
            c, sn = c_ref[src, :], sn_ref[src, :]
            k = _rms(kr[src, :].astype(F32), HEAD_DIM) * kg
            kn_scr[dst, :] = _rope(k, c, sn, DIL_HALF, (0,)).astype(BF16)
            q = _rms(qr[src, :].astype(F32), HEAD_DIM) * qg
            qn_scr[dst, :] = _rope(q, c, sn, DIL_HALF, (0,)).astype(BF16)
            v_scr[dst, :] = vr[src, :].astype(BF16)
            return carry

        lax.fori_loop(0, seq // ck, prep, 0, unroll=2)

        def qblocks(gi, carry, d=d, n=n, tq=tq, kw=kw, nb=nb, og=og, ls=ls, cls=cls):
            uu = [gi * DIL_GROUP + i for i in range(DIL_GROUP)]
            q0 = [(u % nb) * tq for u in uu]
            ks = [jnp.clip(q - DIL_SPAN, 0, n - kw) for q in q0]
            ksl = [pl.ds(pl.multiple_of((u // nb) * n + k0, DIL_SPAN), kw) for u, k0 in zip(uu, ks)]
            s = [lax.dot_general(qn_scr[pl.ds(pl.multiple_of(u * tq, tq), tq), :], kn_scr[ksl[i], :],
                                 (((1,), (1,)), ((), ())), preferred_element_type=F32)
                 for i, u in enumerate(uu)]
            off = (lax.broadcasted_iota(jnp.int32, (tq, kw), 1) - lax.broadcasted_iota(jnp.int32, (tq, kw), 0))
            p, l, lse = [], [], []
            for i in range(DIL_GROUP):
                si = jnp.where(jnp.abs(off + (ks[i] - q0[i])) <= DIL_SPAN, s[i], NEG)
                m = jnp.max(si, axis=-1, keepdims=True)
                pi = jnp.exp(si - m)
                li = jnp.sum(pi, axis=-1, keepdims=True)
                p.append(pi.astype(BF16))
                l.append(li)
                lse.append(m + jnp.log(li))
            for i, u in enumerate(uu):
                start = u // nb + d * q0[i]
                if d == 1:
                    start = pl.multiple_of(start, tq)
                o = jnp.dot(p[i], v_scr[ksl[i], :], preferred_element_type=F32) / l[i]
                og[cls(start, tq), :] = o
                ls[cls(start, tq), :] = jnp.broadcast_to(lse[i], (tq, LANES))
            return carry

        lax.fori_loop(0, seq // tq // DIL_GROUP, qblocks, 0)

    def mix(i, carry):
        sl = pl.ds(pl.multiple_of(i * 256, 256), 256)
        l0, l1, l2 = ls0[sl, :], ls1[sl, :], ls2[sl, :]
        mx = jnp.maximum(jnp.maximum(l0, l1), l2)
        e0, e1, e2 = jnp.exp(l0 - mx), jnp.exp(l1 - mx), jnp.exp(l2 - mx)
        o = (e0 * og0[sl, :] + e1 * og1[sl, :] + e2 * og2[sl, :]) / (e0 + e1 + e2)
        o_ref[sl, :] = o.astype(BF16)
        return carry

    lax.fori_loop(0, seq // 256, mix, 0)


def _dil(zm4, zd4, tabs, qn, kn):
    _, b, s, _ = zm4.shape
    assert s % 256 == 0 and s // DILATIONS[-1] >= DIL_SPAN and (s // 128) % DIL_GROUP == 0
    once = dict(pipeline_mode=pl.Buffered(1))
    main = lambda off: pl.BlockSpec((None, None, s, LANES), lambda bi, hh: (off + hh, bi, 0, 0))
    dil = lambda off: pl.BlockSpec((None, None, s, LANES), lambda bi, hh: (off + hh, bi, 0, 0), **once)
    tab = pl.BlockSpec((s, LANES), lambda bi, hh: (0, 0), **once)
    vec = pl.BlockSpec((1, LANES), lambda bi, hh: (0, 0))
    return pl.pallas_call(
        functools.partial(_dil_kernel, seq=s),
        grid=(b, DIL_HEADS),
        in_specs=[main(B0), main(B0 + 2), main(B0 + 4),
                  dil(0), dil(2), dil(4), dil(6), dil(8), dil(10),
                  tab, tab, vec, vec],
        out_specs=pl.BlockSpec((None, None, s, LANES), lambda bi, hh: (hh, bi, 0, 0)),
        out_shape=jax.ShapeDtypeStruct((DIL_HEADS, b, s, LANES), BF16),
        scratch_shapes=[pltpu.VMEM((s, LANES), BF16)] * 3 + [pltpu.VMEM((s, LANES), F32)] * 6,
        compiler_params=_cparams(("parallel", "parallel")),
        name="mixer_b",
    )(zm4, zm4, zm4, zd4, zd4, zd4, zd4, zd4, zd4, *tabs, qn, kn)


DIFF_HALF = DIFF_QK_DIM // 4 // 2
DIFF_TQ = 512
DIFF_TK = 256
DIFF_SAFE_BOUND = 50.0


def _rms_halves(x):
    lo = lax.broadcasted_iota(jnp.int32, x.shape, 1) < DIFF_QK_DIM
    xx = x * x
    s_lo = jnp.sum(jnp.where(lo, xx, 0.0), axis=-1, keepdims=True)
    s_hi = jnp.sum(jnp.where(lo, 0.0, xx), axis=-1, keepdims=True)
    r = lax.rsqrt(jnp.where(lo, s_lo, s_hi) * (1.0 / DIFF_QK_DIM) + EPS)
    return x * r


def _diff_kernel(q_ref, k_ref, v_ref, c_ref, sn_ref, qn_ref, kn_ref, lam_ref, sub_ref, o_ref,
                 kt_scr, q2_scr, bound_scr, *, seq, tq, tk, lam_init):
    starts = (0, DIFF_QK_DIM)

    @pl.when(pl.program_id(2) == 0)
    def _():
        kg = kn_ref[...]
        qg = qn_ref[...] * (DIFF_QK_DIM ** -0.5 * LOG2E)

        def knorm(i, carry):
            sl = pl.ds(pl.multiple_of(i * tk, tk), tk)
            k = _rms_halves(k_ref[sl, :].astype(F32)) * kg
            k = _rope(k, c_ref[sl, :], sn_ref[sl, :], DIFF_HALF, starts)
            kt_scr[i] = k.T.astype(BF16)
            return carry

        lax.fori_loop(0, seq // tk, knorm, 0, unroll=2)

        def qnorm(i, carry):
            sl = pl.ds(pl.multiple_of(i * tq, tq), tq)
            q = _rms_halves(q_ref[sl, :].astype(F32)) * qg
            q = _rope(q, c_ref[sl, :], sn_ref[sl, :], DIFF_HALF, starts)
            lo = lax.broadcasted_iota(jnp.int32, q.shape, 1) < DIFF_QK_DIM
            q2_scr[i, 0:tq, :] = jnp.where(lo, q, 0.0).astype(BF16)
            q2_scr[i, tq:2 * tq, :] = jnp.where(lo, 0.0, q).astype(BF16)
            return carry

        lax.fori_loop(0, seq // tq, qnorm, 0, unroll=2)

        gmax = jnp.max(jnp.abs(qg)) * jnp.max(jnp.abs(kg))
        bound_scr[0] = gmax * (DIFF_QK_DIM * 1.01)

    nk = seq // tk
    lane_sums = lambda p: functools.reduce(jnp.add, [p[:, c * LANES:(c + 1) * LANES] for c in range(tk // LANES)])

    def finish(acc, l):
        lv = lam_ref[...]
        lam = (jnp.exp(jnp.sum(lv[0:1] * lv[1:2], axis=-1, keepdims=True))
               - jnp.exp(jnp.sum(lv[2:3] * lv[3:4], axis=-1, keepdims=True)) + lam_init)
        on = acc / jnp.sum(l, axis=-1, keepdims=True)
        o = on[:tq] - lam * on[tq:]
        o_ref[...] = (_rms(o, HEAD_DIM) * sub_ref[...] * (1.0 - lam_init)).astype(BF16)

    def attend(shift):
        q2 = q2_scr[pl.program_id(2)]
        m = jnp.full((2 * tq, 1), -jnp.inf, F32)
        l = jnp.zeros((2 * tq, LANES), F32)
        acc = jnp.zeros((2 * tq, HEAD_DIM), F32)
        s_next = jnp.dot(q2, kt_scr[0], preferred_element_type=F32)
        for j in range(nk):
            s = s_next
            if j + 1 < nk:
                s_next = jnp.dot(q2, kt_scr[j + 1], preferred_element_type=F32)
            v = v_ref[j * tk:(j + 1) * tk, :]
            if shift is None:
                m_new = jnp.maximum(m, jnp.max(s, axis=-1, keepdims=True))
                alpha = jnp.exp2(m - m_new)
                p = jnp.exp2(s - m_new)
                l = alpha * l + lane_sums(p)
                acc = alpha * acc + jnp.dot(p.astype(BF16), v, preferred_element_type=F32)
                m = m_new
            else:
                p = jnp.exp2(s - shift)
                l = l + lane_sums(p)
                acc = acc + jnp.dot(p.astype(BF16), v, preferred_element_type=F32)
        finish(acc, l)

    bound = bound_scr[0]
    safe = bound <= DIFF_SAFE_BOUND

    @pl.when(safe)
    def _():
        attend(bound)

    @pl.when(jnp.logical_not(safe))
    def _():
        attend(None)


def _diff(z4, tabs, qn, kn, lamv, sub, lam_init):
    _, b, s, _ = z4.shape
    tq = min(DIFF_TQ, s)
    tk = min(DIFF_TK, s)
    assert s % tq == 0 and s % tk == 0
    kv = lambda off: pl.BlockSpec((None, None, s, LANES), lambda bi, h, qi: (off + h, bi, 0, 0))
    tab = pl.BlockSpec((s, LANES), lambda bi, h, qi: (0, 0))
    vec = pl.BlockSpec((1, LANES), lambda bi, h, qi: (0, 0))
    return pl.pallas_call(
        functools.partial(_diff_kernel, seq=s, tq=tq, tk=tk, lam_init=lam_init),
        grid=(b, DIFF_HEADS, s // tq),
        in_specs=[
            kv(QC), kv(KC), kv(VC), tab, tab, vec, vec,
            pl.BlockSpec((4, DIFF_QK_DIM), lambda bi, h, qi: (0, 0)),
            vec,
        ],
        out_specs=pl.BlockSpec((None, None, tq, LANES), lambda bi, h, qi: (h, bi, qi, 0)),
        out_shape=jax.ShapeDtypeStruct((DIFF_HEADS, b, s, LANES), BF16),
        scratch_shapes=[pltpu.VMEM((s // tk, LANES, tk), BF16), pltpu.VMEM((s // tq, 2 * tq, LANES), BF16),
                        pltpu.SMEM((1,), F32)],
        compiler_params=_cparams(("parallel", "parallel", "arbitrary")),
        name="mixer_c",
    )(z4, z4, z4, *tabs, qn, kn, lamv, sub)


def _merge_kernel(x_ref, g_ref, sc_ref, sh_ref, oa_ref, ob_ref, oc_ref, wa_ref, wb_ref, wc_ref,
                  wga_ref, wgb_ref, wgc_ref, bga_ref, bgb_ref, bgc_ref, m_ref, h_scr):
    @pl.when(pl.program_id(1) == 0)
    def _():
        _modnorm_to(h_scr, x_ref, g_ref, sc_ref, sh_ref)

    h = h_scr[...]
    cat = lambda ref: jnp.concatenate([ref[i] for i in range(ref.shape[0])], axis=-1)

    def gate(w_ref, b_ref):
        t = jnp.dot(h, w_ref[...], preferred_element_type=F32) + b_ref[...]
        return 1.0 / (1.0 + jnp.exp(-t))

    y = gate(wga_ref, bga_ref) * jnp.dot(cat(oa_ref), wa_ref[...], preferred_element_type=F32)
    y = y + gate(wgb_ref, bgb_ref) * jnp.dot(cat(ob_ref), wb_ref[...], preferred_element_type=F32)
    y = y + gate(wgc_ref, bgc_ref) * jnp.dot(cat(oc_ref), wc_ref[...], preferred_element_type=F32)
    m_ref[...] = y.astype(BF16)


def _merge(x2, seq, g, sc, sh, oa, ob, oc, wa, wb, wc, wg, bg):
    t, d = x2.shape
    tm = _pick(seq, 1024)
    tn = _pick(d, 512)
    per = seq // tm
    nj = d // tn
    heads = lambda nh: pl.BlockSpec((nh, tm, LANES), lambda i, j: (0, i, 0))
    wcol = lambda k: pl.BlockSpec((k, tn), lambda i, j: (0, j))
    gcol = lambda sec: pl.BlockSpec((d, tn), lambda i, j: (0, sec * nj + j))
    bcol = lambda sec: pl.BlockSpec((1, tn), lambda i, j: (0, sec * nj + j))
    modspec = pl.BlockSpec((None, 1, d), lambda i, j: (i // per, 0, 0))
    bg2 = bg.reshape(1, 3 * d)
    return pl.pallas_call(
        _merge_kernel,
        grid=(t // tm, nj),
        in_specs=[
            pl.BlockSpec((tm, d), lambda i, j: (i, 0)), pl.BlockSpec((1, d), lambda i, j: (0, 0)),
            modspec, modspec,
            heads(NA_HEADS), heads(DIL_HEADS), heads(DIFF_HEADS),
            wcol(wa.shape[0]), wcol(wb.shape[0]), wcol(wc.shape[0]),
            gcol(0), gcol(1), gcol(2), bcol(0), bcol(1), bcol(2),
        ],
        out_specs=pl.BlockSpec((tm, tn), lambda i, j: (i, j)),
        out_shape=jax.ShapeDtypeStruct((t, d), BF16),
        scratch_shapes=[pltpu.VMEM((tm, d), BF16)],
        compiler_params=_cparams(("parallel", "arbitrary")),
        name="merge",
    )(x2, g, sc, sh, oa, ob, oc, wa, wb, wc, wg, wg, wg, bg2, bg2, bg2)


def _outproj_kernel(m_ref, w_ref, x_ref, g1_ref, o_ref):
    y = jnp.dot(m_ref[...], w_ref[...], preferred_element_type=F32)
    o_ref[...] = x_ref[...] + g1_ref[...] * y


def _outproj(m, w, x2, seq, g1):
    t, d = x2.shape
    tm = _pick(seq, 1024)
    tn = _pick(d, 1024)
    per = seq // tm
    return pl.pallas_call(
        _outproj_kernel,
        grid=(t // tm, d // tn),
        in_specs=[
            pl.BlockSpec((tm, d), lambda i, j: (i, 0)),
            pl.BlockSpec((d, tn), lambda i, j: (0, j)),
            pl.BlockSpec((tm, tn), lambda i, j: (i, j)),
            pl.BlockSpec((None, 1, tn), lambda i, j: (i // per, 0, j)),
        ],
        out_specs=pl.BlockSpec((tm, tn), lambda i, j: (i, j)),
        out_shape=jax.ShapeDtypeStruct((t, d), F32),
        compiler_params=_cparams(("parallel", "parallel")),
        name="out_proj",
    )(m, w, x2, g1)


MLP_ACC_COLS = 512


def _mlp_kernel(x_ref, g_ref, sc_ref, sh_ref, g2_ref, w1_ref, w2_ref, o_ref, h_scr):
    k = pl.program_id(1)

    @pl.when(k == 0)
    def _():
        _modnorm_to(h_scr, x_ref, g_ref, sc_ref, sh_ref)

        o_ref[...] = jnp.zeros_like(o_ref)

    u = jnp.maximum(jnp.dot(h_scr[...], w1_ref[...], preferred_element_type=F32), 0.0)
    u = (u * u).astype(BF16)
    d = o_ref.shape[1]
    nc = _pick(d, MLP_ACC_COLS)
    for c in range(d // nc):
        cs = slice(c * nc, (c + 1) * nc)
        o_ref[:, cs] += jnp.dot(u, w2_ref[:, cs], preferred_element_type=F32)

    @pl.when(k == pl.num_programs(1) - 1)
    def _():
        o_ref[...] = x_ref[...] + g2_ref[...] * o_ref[...]


def _mlp(x2, seq, g, sc, sh, g2, w1, w2):
    t, d = x2.shape
    ff = w1.shape[1]
    tm = _pick(seq, 1024)
    tf = _pick(ff, 512)
    per = seq // tm
    modspec = pl.BlockSpec((None, 1, d), lambda i, k: (i // per, 0, 0))
    return pl.pallas_call(
        _mlp_kernel,
        grid=(t // tm, ff // tf),
        in_specs=[
            pl.BlockSpec((tm, d), lambda i, k: (i, 0)),
            pl.BlockSpec((1, d), lambda i, k: (0, 0)),
            modspec, modspec, modspec,
            pl.BlockSpec((d, tf), lambda i, k: (0, k)),
            pl.BlockSpec((tf, d), lambda i, k: (k, 0)),
        ],
        out_specs=pl.BlockSpec((tm, d), lambda i, k: (i, 0)),
        out_shape=jax.ShapeDtypeStruct((t, d), F32),
        scratch_shapes=[pltpu.VMEM((tm, d), BF16)],
        compiler_params=_cparams(("parallel", "arbitrary")),
        name="mlp",
    )(x2, g, sc, sh, g2, w1, w2)


def _layer(x2, b, s, mods, layer_idx, p, tabs_b, tabs_c):
    d = x2.shape[1]
    sh1, sc1, g1, sh2, sc2, g2 = [m.reshape(b, 1, d) for m in jnp.split(mods, 6, axis=-1)]
    zm = _inproj(x2, s, p["norm1_g"], sc1, sh1, p["w_in_main"], BF16, "in_proj_main")
    zd = _inproj(x2, s, p["norm1_g"], sc1, sh1, p["w_in_dil"], F32, "in_proj_dil")
    zm4 = zm.reshape(zm.shape[0], b, s, LANES)
    zd4 = zd.reshape(zd.shape[0], b, s, LANES)
    oa = _na(zm4, p["bias_a"], p["qn_a"], p["kn_a"]).reshape(NA_HEADS, b * s, LANES)
    ob = _dil(zm4, zd4, tabs_b, p["qn_b"], p["kn_b"]).reshape(DIL_HEADS, b * s, LANES)
    lam_init = 0.8 - 0.6 * math.exp(-0.3 * layer_idx)
    oc = _diff(zm4, tabs_c, p["qn_c"], p["kn_c"], p["lamv"], p["subln_c"], lam_init)
    oc = oc.reshape(DIFF_HEADS, b * s, LANES)
    m = _merge(x2, s, p["norm1_g"], sc1, sh1, oa, ob, oc,
               p["w_br_a"], p["w_br_b"], p["w_br_c"], p["w_gate"], p["b_gate"])
    x2 = _outproj(m, p["w_out"], x2, s, g1)
    return _mlp(x2, s, p["norm2_g"], sc2, sh2, g2, p["w_ff1"], p["w_ff2"])


def kernel(x_prompt, x_sample, c_prompt, c_sample, norm1_g, w_ada, b_ada, w_in, qn_a, kn_a, rpb_a, qn_b, kn_b, qn_c, kn_c, lam_q1, lam_k1, lam_q2, lam_k2, subln_c, w_br_a, w_br_b, w_br_c, w_gate, b_gate, w_out, norm2_g, w_ff1, w_ff2):
    depth = w_in.shape[0]
    bp, sp, d = x_prompt.shape
    bs, ss, _ = x_sample.shape
    mods = _ada(jnp.concatenate([c_prompt, c_sample], axis=0), w_ada, b_ada)

    cat2 = lambda v: jnp.concatenate([v, v], axis=-1)
    tabs = {s: (_rope_tables(s, DIL_HALF, (0,)), _rope_tables(s, DIFF_HALF, (0, DIFF_QK_DIM)))
            for s in {sp, ss}}
    xp = x_prompt.reshape(bp * sp, d)
    xs = x_sample.reshape(bs * ss, d)
    for l in range(depth):
        w_in_main, w_in_dil = _split_w_in(w_in[l].astype(BF16))
        p = {
            "norm1_g": norm1_g[l].reshape(1, d), "norm2_g": norm2_g[l].reshape(1, d),
            "w_in_main": w_in_main, "w_in_dil": w_in_dil,
            "w_gate": w_gate[l].astype(BF16), "b_gate": b_gate[l],
            "w_br_a": w_br_a[l].astype(BF16), "w_br_b": w_br_b[l].astype(BF16),
            "w_br_c": w_br_c[l].astype(BF16), "w_out": w_out[l].astype(BF16),
            "w_ff1": w_ff1[l].astype(BF16), "w_ff2": w_ff2[l].astype(BF16),
            "qn_a": qn_a[l].reshape(1, LANES), "kn_a": kn_a[l].reshape(1, LANES),
            "qn_b": qn_b[l].reshape(1, LANES), "kn_b": kn_b[l].reshape(1, LANES),
            "qn_c": cat2(qn_c[l]).reshape(1, LANES), "kn_c": cat2(kn_c[l]).reshape(1, LANES),
            "lamv": jnp.stack([lam_q1[l], lam_k1[l], lam_q2[l], lam_k2[l]]).astype(F32),
            "subln_c": subln_c[l].reshape(1, LANES),
            "bias_a": _na_bias(rpb_a[l]),
        }
        xp = _layer(xp, bp, sp, mods[l, :bp], l, p, *tabs[sp])
        xs = _layer(xs, bs, ss, mods[l, bp:], l, p, *tabs[ss])
    return xp.reshape(bp, sp, d), xs.reshape(bs, ss, d)
```

```python
import functools
import math

import jax
import jax.numpy as jnp
import numpy as np
from jax import lax
from jax.experimental import pallas as pl
from jax.experimental.pallas import tpu as pltpu

F32 = jnp.float32
BF16 = jnp.bfloat16

LANES = 128
HEAD_DIM = 128
ROPE_THETA = 500000.0
EPS = 1e-6
NEG = -1e30
LOG2E = 1.4426950408889634
GRID_W = 64
WIN_ROWS = 8
WIN_COLS = 16
NA_HEADS = 4
DILATIONS = (1, 4, 16)
DIL_HEADS = 2
DIL_SPAN = 64
DIFF_HEADS = 6
DIFF_QK_DIM = 64
QA, KA, VA, B0, QC, KC, VC = 0, 4, 8, 12, 18, 24, 30

VMEM_LIMIT = 56 * 1024 * 1024


def _split_w_in(w):
    cols = lambda a, b: w[:, a * LANES:b * LANES]
    grp = lambda g: [cols(12 + 2 * g, 14 + 2 * g), cols(18 + 2 * g, 20 + 2 * g), cols(24 + 2 * g, 26 + 2 * g)]
    main = jnp.concatenate([cols(0, 12)] + grp(0) + [cols(30, 48)], axis=1)
    dil = jnp.concatenate(grp(1) + grp(2), axis=1)
    return main, dil


def _cparams(sem, vmem=VMEM_LIMIT):
    return pltpu.CompilerParams(dimension_semantics=sem, vmem_limit_bytes=vmem)


def _pick(n, pref):
    t = min(n, pref)
    while n % t:
        t //= 2
    return t


def _rms(x, width):
    return x * lax.rsqrt(jnp.sum(x * x, axis=-1, keepdims=True) * (1.0 / width) + EPS)


def _ada_kernel(c_ref, w_ref, b_ref, o_ref):
    c = c_ref[...]
    a = c * (1.0 / (1.0 + jnp.exp(-c)))
    o_ref[0] = jnp.dot(a.astype(BF16), w_ref[0].astype(BF16), preferred_element_type=F32) + b_ref[0]


def _ada(c_all, w_ada, b_ada):
    depth, d, n = w_ada.shape
    nb = c_all.shape[0]
    tn = _pick(n, 1024)
    return pl.pallas_call(
        _ada_kernel,
        grid=(depth, n // tn),
        in_specs=[
            pl.BlockSpec((nb, d), lambda l, j: (0, 0)),
            pl.BlockSpec((1, d, tn), lambda l, j: (l, 0, j)),
            pl.BlockSpec((1, 1, tn), lambda l, j: (l, 0, j)),
        ],
        out_specs=pl.BlockSpec((1, nb, tn), lambda l, j: (l, 0, j)),
        out_shape=jax.ShapeDtypeStruct((depth, nb, n), F32),
        compiler_params=_cparams(("parallel", "parallel")),
        name="ada_mod",
    )(c_all, w_ada, b_ada.reshape(depth, 1, n))


NORM_ROWS = 64


def _modnorm_to(h_scr, x_ref, g_ref, sc_ref, sh_ref):
    tm, d = x_ref.shape
    gs = g_ref[...] * (1.0 + sc_ref[...])
    sh = sh_ref[...]

    def body(i, carry):
        sl = pl.ds(pl.multiple_of(i * NORM_ROWS, NORM_ROWS), NORM_ROWS)
        x = x_ref[sl, :]
        h_scr[sl, :] = (_rms(x, d) * gs + sh).astype(BF16)
        return carry

    lax.fori_loop(0, tm // NORM_ROWS, body, 0, unroll=4)


def _inproj_kernel(x_ref, g_ref, sc_ref, sh_ref, w_ref, z_ref, h_scr):
    @pl.when(pl.program_id(1) == 0)
    def _():
        _modnorm_to(h_scr, x_ref, g_ref, sc_ref, sh_ref)

    acc = jnp.dot(h_scr[...], w_ref[...], preferred_element_type=F32)
    for c in range(z_ref.shape[0]):
        z_ref[c] = acc[:, c * LANES:(c + 1) * LANES].astype(z_ref.dtype)


def _inproj(x2, seq, g, sc, sh, w, out_dtype, name):
    t, d = x2.shape
    n = w.shape[1]
    tm = _pick(seq, 1024)
    tn = _pick(n, 1536)
    per = seq // tm
    return pl.pallas_call(
        _inproj_kernel,
        grid=(t // tm, n // tn),
        in_specs=[
            pl.BlockSpec((tm, d), lambda i, j: (i, 0)),
            pl.BlockSpec((1, d), lambda i, j: (0, 0)),
            pl.BlockSpec((None, 1, d), lambda i, j: (i // per, 0, 0)),
            pl.BlockSpec((None, 1, d), lambda i, j: (i // per, 0, 0)),
            pl.BlockSpec((d, tn), lambda i, j: (0, j)),
        ],
        out_specs=pl.BlockSpec((tn // LANES, tm, LANES), lambda i, j: (j, i, 0)),
        out_shape=jax.ShapeDtypeStruct((n // LANES, t, LANES), out_dtype),
        scratch_shapes=[pltpu.VMEM((tm, d), BF16)],
        compiler_params=_cparams(("parallel", "arbitrary")),
        name=name,
    )(x2, g, sc, sh, w)


def _na_bias(rpb):
    c = jnp.arange(GRID_W)
    cs = jnp.clip(c - WIN_COLS // 2, 0, GRID_W - WIN_COLS)
    kc = jnp.arange(GRID_W)
    inwin = (kc[None, :] >= cs[:, None]) & (kc[None, :] < cs[:, None] + WIN_COLS)
    dc = kc[None, :] - c[:, None] + WIN_COLS - 1
    oh_c = (dc[:, :, None] == jnp.arange(2 * WIN_COLS - 1)).astype(F32)
    dr = jnp.arange(WIN_ROWS)[None, :] - jnp.arange(WIN_ROWS)[:, None] + WIN_ROWS - 1
    oh_r = (dr[:, :, None] == jnp.arange(2 * WIN_ROWS - 1)).astype(F32)
    b = jnp.einsum("hij,dri,ckj->hdcrk", rpb.astype(F32), oh_r, oh_c, precision=lax.Precision.HIGHEST)
    b = jnp.where(inwin[None, None, :, None, :], b, NEG)
    return b.reshape(rpb.shape[0], WIN_ROWS, GRID_W, WIN_ROWS * GRID_W)


NA_GROUP = 8


def _na_kernel(q_ref, k_ref, v_ref, bias_ref, qn_ref, kn_ref, o_ref, qn_scr, kn_scr, *, rows):
    band = WIN_ROWS * GRID_W
    kg = kn_ref[...]
    qg = qn_ref[...] * HEAD_DIM ** -0.5

    def norm(i, carry):
        sl = pl.ds(pl.multiple_of(i * 256, 256), 256)
        kn_scr[sl, :] = (_rms(k_ref[sl, :].astype(F32), HEAD_DIM) * kg).astype(BF16)
        qn_scr[sl, :] = (_rms(q_ref[sl, :].astype(F32), HEAD_DIM) * qg).astype(BF16)
        return carry

    lax.fori_loop(0, rows * GRID_W // 256, norm, 0, unroll=2)

    def body(g, carry):
        rr = [g * NA_GROUP + i for i in range(NA_GROUP)]
        rs = [jnp.clip(r - WIN_ROWS // 2, 0, rows - WIN_ROWS) for r in rr]
        qsl = [pl.ds(pl.multiple_of(r * GRID_W, GRID_W), GRID_W) for r in rr]
        ksl = [pl.ds(pl.multiple_of(r0 * GRID_W, GRID_W), band) for r0 in rs]
        s = [lax.dot_general(qn_scr[qsl[i], :], kn_scr[ksl[i], :], (((1,), (1,)), ((), ())),
                             preferred_element_type=F32) for i in range(NA_GROUP)]
        p, l = [], []
        for i in range(NA_GROUP):
            si = s[i] + bias_ref[rr[i] - rs[i]]
            pi = jnp.exp(si - jnp.max(si, axis=-1, keepdims=True))
            l.append(jnp.sum(pi, axis=-1, keepdims=True))
            p.append(pi.astype(BF16))
        for i in range(NA_GROUP):
            o = jnp.dot(p[i], v_ref[ksl[i], :], preferred_element_type=F32) / l[i]
            o_ref[qsl[i], :] = o.astype(BF16)
        return carry

    lax.fori_loop(0, rows // NA_GROUP, body, 0)


def _na(z4, bias, qn, kn):
    _, b, s, _ = z4.shape
    rows = s // GRID_W
    assert rows >= WIN_ROWS and rows % NA_GROUP == 0 and s % 256 == 0
    slab = lambda off: pl.BlockSpec((None, None, s, LANES), lambda bi, h: (off + h, bi, 0, 0))
    return pl.pallas_call(
        functools.partial(_na_kernel, rows=rows),
        grid=(b, NA_HEADS),
        in_specs=[
            slab(QA), slab(KA), slab(VA),
            pl.BlockSpec((None, WIN_ROWS, GRID_W, WIN_ROWS * GRID_W), lambda bi, h: (h, 0, 0, 0)),
            pl.BlockSpec((1, LANES), lambda bi, h: (0, 0)),
            pl.BlockSpec((1, LANES), lambda bi, h: (0, 0)),
        ],
        out_specs=pl.BlockSpec((None, None, s, LANES), lambda bi, h: (h, bi, 0, 0)),
        out_shape=jax.ShapeDtypeStruct((NA_HEADS, b, s, LANES), BF16),
        scratch_shapes=[pltpu.VMEM((s, LANES), BF16), pltpu.VMEM((s, LANES), BF16)],
        compiler_params=_cparams(("parallel", "parallel")),
        name="mixer_a",
    )(z4, z4, z4, bias, qn, kn)


def _rope_tables(seq, half, starts):
    cos, sin = _rope_cos_sin(seq, half)
    c = np.ones((seq, LANES), np.float32)
    sn = np.zeros((seq, LANES), np.float32)
    for st in starts:
        c[:, st:st + half] = cos
        c[:, st + half:st + 2 * half] = cos
        sn[:, st:st + half] = -sin
        sn[:, st + half:st + 2 * half] = sin
    return jnp.asarray(c), jnp.asarray(sn)


def _rope_cos_sin(seq, half):
    inv = (np.float32(ROPE_THETA) ** (-np.arange(half, dtype=np.float32) / np.float32(half))).astype(np.float32)
    ang = (np.arange(seq, dtype=np.float32)[:, None] * inv[None, :]).astype(np.float64)
    return np.cos(ang).astype(np.float32), np.sin(ang).astype(np.float32)


def _rope_tables_t(seq, half, chunk):
    cos, sin = _rope_cos_sin(seq, half)
    c = np.concatenate([cos, cos], axis=1).T
    sn = np.concatenate([-sin, sin], axis=1).T
    split = lambda t: jnp.asarray(t.reshape(2 * half, seq // chunk, chunk).transpose(1, 0, 2))
    return split(c), split(sn)


def _rope(x, c, sn, half, starts):
    lane = lax.broadcasted_iota(jnp.int32, x.shape, 1)
    first = functools.reduce(jnp.logical_or, [(lane >= st) & (lane < st + half) for st in starts])
    rot = jnp.where(first, pltpu.roll(x, LANES - half, 1), pltpu.roll(x, half, 1))
    return x * c + rot * sn


DIL_HALF = HEAD_DIM // 4 // 2
DIL_GROUP = 4


def _dil_kernel(q0_ref, k0_ref, v0_ref, q1_ref, k1_ref, v1_ref, q2_ref, k2_ref, v2_ref,
                c_ref, sn_ref, qn_ref, kn_ref, o_ref,
                qs0, ks0, vs0, qs1, ks1, vs1, qs2, ks2, vs2, og0, og1, og2, ls0, ls1, ls2, *, seq):
    kg = kn_ref[...]
    qg = qn_ref[...] * HEAD_DIM ** -0.5
    groups = ((q0_ref, k0_ref, v0_ref, qs0, ks0, vs0, og0, ls0), (q1_ref, k1_ref, v1_ref, qs1, ks1, vs1, og1, ls1),
              (q2_ref, k2_ref, v2_ref, qs2, ks2, vs2, og2, ls2))
    preps, attns = [], []

    for (qr, kr, vr, qn_scr, kn_scr, v_scr, og, ls), d in zip(groups, DILATIONS):
        n = seq // d
        tq = min(128, n)
        kw = min(2 * tq, n)
        ck = min(256, n)
        nb = n // tq
        nc = n // ck

        def cls(start, size, d=d):
            return pl.ds(start, size) if d == 1 else pl.ds(start, size, stride=d)

        def prep(u, d=d, ck=ck, nc=nc, qr=qr, kr=kr, vr=vr, qn_scr=qn_scr, kn_scr=kn_scr, v_scr=v_scr, cls=cls):
            start = u // nc + d * ck * (u % nc)
            if d == 1:
                start = pl.multiple_of(start, ck)
            src = cls(start, ck)
            dst = pl.ds(pl.multiple_of(u * ck, ck), ck)
            c, sn = c_ref[src, :], sn_ref[src, :]
            k = _rms(kr[src, :].astype(F32), HEAD_DIM) * kg
            kn_scr[dst, :] = _rope(k, c, sn, DIL_HALF, (0,)).astype(BF16)
            q = _rms(qr[src, :].astype(F32), HEAD_DIM) * qg
            qn_scr[dst, :] = _rope(q, c, sn, DIL_HALF, (0,)).astype(BF16)
            v_scr[dst, :] = vr[src, :].astype(BF16)

        def qblocks(gi, d=d, n=n, tq=tq, kw=kw, nb=nb, qn_scr=qn_scr, kn_scr=kn_scr, v_scr=v_scr,
                    og=og, ls=ls, cls=cls):
            uu = [gi * DIL_GROUP + i for i in range(DIL_GROUP)]
            q0 = [(u % nb) * tq for u in uu]
            ks = [jnp.clip(q - DIL_SPAN, 0, n - kw) for q in q0]
            ksl = [pl.ds(pl.multiple_of((u // nb) * n + k0, DIL_SPAN), kw) for u, k0 in zip(uu, ks)]
            s = [lax.dot_general(qn_scr[pl.ds(pl.multiple_of(u * tq, tq), tq), :], kn_scr[ksl[i], :],
                                 (((1,), (1,)), ((), ())), preferred_element_type=F32)
                 for i, u in enumerate(uu)]
            off = (lax.broadcasted_iota(jnp.int32, (tq, kw), 1) - lax.broadcasted_iota(jnp.int32, (tq, kw), 0))
            p, l, lse = [], [], []
            for i in range(DIL_GROUP):
                si = jnp.where(jnp.abs(off + (ks[i] - q0[i])) <= DIL_SPAN, s[i], NEG)
                m = jnp.max(si, axis=-1, keepdims=True)
                pi = jnp.exp(si - m)
                li = jnp.sum(pi, axis=-1, keepdims=True)
                p.append(pi.astype(BF16))
                l.append(li)
                lse.append(m + jnp.log(li))
            for i, u in enumerate(uu):
                start = u // nb + d * q0[i]
                if d == 1:
                    start = pl.multiple_of(start, tq)
                o = jnp.dot(p[i], v_scr[ksl[i], :], preferred_element_type=F32) / l[i]
                og[cls(start, tq), :] = o
                ls[cls(start, tq), :] = jnp.broadcast_to(lse[i], (tq, LANES))

        preps.append((prep, seq // ck))
        attns.append((qblocks, seq // tq // DIL_GROUP))

    def stage(prep, attn):
        if attn is None:
            fn, trips = prep
            return lax.fori_loop(0, trips, lambda u, c: (fn(u), c)[1], 0, unroll=2)
        afn, trips = attn
        if prep is None:
            return lax.fori_loop(0, trips, lambda g, c: (afn(g), c)[1], 0)
        pfn, chunks = prep
        assert chunks % trips == 0
        per = chunks // trips

        def body(g, carry):
            for i in range(per):
                pfn(g * per + i)
            afn(g)
            return carry

        return lax.fori_loop(0, trips, body, 0)

    stage(preps[0], None)
    stage(preps[1], attns[0])
    stage(preps[2], attns[1])
    stage(None, attns[2])

    def mix(i, carry):
        sl = pl.ds(pl.multiple_of(i * 256, 256), 256)
        l0, l1, l2 = ls0[sl, :], ls1[sl, :], ls2[sl, :]
        mx = jnp.maximum(jnp.maximum(l0, l1), l2)
        e0, e1, e2 = jnp.exp(l0 - mx), jnp.exp(l1 - mx), jnp.exp(l2 - mx)
        o = (e0 * og0[sl, :] + e1 * og1[sl, :] + e2 * og2[sl, :]) / (e0 + e1 + e2)
        o_ref[sl, :] = o.astype(BF16)
        return carry

    lax.fori_loop(0, seq // 256, mix, 0)


def _dil(zm4, zd4, tabs, qn, kn):
    _, b, s, _ = zm4.shape
    assert s % 256 == 0 and s // DILATIONS[-1] >= DIL_SPAN and (s // 128) % DIL_GROUP == 0
    once = dict(pipeline_mode=pl.Buffered(1))
    main = lambda off: pl.BlockSpec((None, None, s, LANES), lambda bi, hh: (off + hh, bi, 0, 0))
    dil = lambda off: pl.BlockSpec((None, None, s, LANES), lambda bi, hh: (off + hh, bi, 0, 0), **once)
    tab = pl.BlockSpec((s, LANES), lambda bi, hh: (0, 0), **once)
    vec = pl.BlockSpec((1, LANES), lambda bi, hh: (0, 0))
    return pl.pallas_call(
        functools.partial(_dil_kernel, seq=s),
        grid=(b, DIL_HEADS),
        in_specs=[main(B0), main(B0 + 2), main(B0 + 4),
                  dil(0), dil(2), dil(4), dil(6), dil(8), dil(10),
                  tab, tab, vec, vec],
        out_specs=pl.BlockSpec((None, None, s, LANES), lambda bi, hh: (hh, bi, 0, 0)),
        out_shape=jax.ShapeDtypeStruct((DIL_HEADS, b, s, LANES), BF16),
        scratch_shapes=[pltpu.VMEM((s, LANES), BF16)] * 9 + [pltpu.VMEM((s, LANES), F32)] * 6,
        compiler_params=_cparams(("parallel", "parallel")),
        name="mixer_b",
    )(zm4, zm4, zm4, zd4, zd4, zd4, zd4, zd4, zd4, *tabs, qn, kn)


DIFF_HALF = DIFF_QK_DIM // 4 // 2
DIFF_TQ = 512
DIFF_TK = 256
DIFF_SAFE_BOUND = 50.0


def _norm_rope_t(x, gt, ct, snt):
    rows = x.shape[0]
    xt = x.T
    g = jnp.concatenate([gt] * (rows // LANES), axis=1)
    w = 2 * DIFF_HALF

    def comp(y, gc):
        r = lax.rsqrt(jnp.sum(y * y, axis=0, keepdims=True) * (1.0 / DIFF_QK_DIM) + EPS)
        y = y * r * gc
        head = y[:w]
        rot = jnp.concatenate([head[DIFF_HALF:], head[:DIFF_HALF]], axis=0)
        return jnp.concatenate([head * ct + rot * snt, y[w:]], axis=0)

    h = DIFF_QK_DIM
    return jnp.concatenate([comp(xt[:h], g[:h]), comp(xt[h:], g[h:])], axis=0)


def _diff_kernel(q_ref, k_ref, v_ref, ct_ref, snt_ref, qgt_ref, kgt_ref, lam_ref, sub_ref, o_ref,
                 kt_scr, q2_scr, bound_scr, *, seq, tq, tk, lam_init):
    @pl.when(pl.program_id(2) == 0)
    def _():
        kg = kgt_ref[...]
        qg = qgt_ref[...] * (DIFF_QK_DIM ** -0.5 * LOG2E)

        def knorm(i, carry):
            sl = pl.ds(pl.multiple_of(i * tk, tk), tk)
            kt_scr[i] = _norm_rope_t(k_ref[sl, :].astype(F32), kg, ct_ref[i], snt_ref[i]).astype(BF16)
            return carry

        lax.fori_loop(0, seq // tk, knorm, 0, unroll=4)

        per = tq // tk

        def qnorm(i, carry):
            sl = pl.ds(pl.multiple_of(i * tk, tk), tk)
            q = _norm_rope_t(q_ref[sl, :].astype(F32), qg, ct_ref[i], snt_ref[i]).T
            lo = lax.broadcasted_iota(jnp.int32, q.shape, 1) < DIFF_QK_DIM
            blk = i // per
            off = (i % per) * tk
            q2_scr[blk, pl.ds(pl.multiple_of(off, tk), tk), :] = jnp.where(lo, q, 0.0).astype(BF16)
            q2_scr[blk, pl.ds(pl.multiple_of(tq + off, tk), tk), :] = jnp.where(lo, 0.0, q).astype(BF16)
            return carry

        lax.fori_loop(0, seq // tk, qnorm, 0, unroll=4)

        gmax = jnp.max(jnp.abs(qg)) * jnp.max(jnp.abs(kg))
        bound_scr[0] = gmax * (DIFF_QK_DIM * 1.01)

    nk = seq // tk
    lane_sums = lambda p: functools.reduce(jnp.add, [p[:, c * LANES:(c + 1) * LANES] for c in range(tk // LANES)])

    def finish(acc, l):
        lv = lam_ref[...]
        lam = (jnp.exp(jnp.sum(lv[0:1] * lv[1:2], axis=-1, keepdims=True))
               - jnp.exp(jnp.sum(lv[2:3] * lv[3:4], axis=-1, keepdims=True)) + lam_init)
        on = acc / jnp.sum(l, axis=-1, keepdims=True)
        o = on[:tq] - lam * on[tq:]
        o_ref[...] = (_rms(o, HEAD_DIM) * sub_ref[...] * (1.0 - lam_init)).astype(BF16)

    def attend(shift):
        q2 = q2_scr[pl.program_id(2)]
        m = jnp.full((2 * tq, 1), -jnp.inf, F32)
        l = jnp.zeros((2 * tq, LANES), F32)
        acc = jnp.zeros((2 * tq, HEAD_DIM), F32)
        s_next = jnp.dot(q2, kt_scr[0], preferred_element_type=F32)
        for j in range(nk):
            s = s_next
            if j + 1 < nk:
                s_next = jnp.dot(q2, kt_scr[j + 1], preferred_element_type=F32)
            v = v_ref[j * tk:(j + 1) * tk, :]
            if shift is None:
                m_new = jnp.maximum(m, jnp.max(s, axis=-1, keepdims=True))
                alpha = jnp.exp2(m - m_new)
                p = jnp.exp2(s - m_new)
                l = alpha * l + lane_sums(p)
                acc = alpha * acc + jnp.dot(p.astype(BF16), v, preferred_element_type=F32)
                m = m_new
            else:
                p = jnp.exp2(s - shift)
                l = l + lane_sums(p)
                acc = acc + jnp.dot(p.astype(BF16), v, preferred_element_type=F32)
        finish(acc, l)

    bound = bound_scr[0]
    safe = bound <= DIFF_SAFE_BOUND

    @pl.when(safe)
    def _():
        attend(bound)

    @pl.when(jnp.logical_not(safe))
    def _():
        attend(None)


def _diff(z4, qn, kn, lamv, sub, lam_init):
    _, b, s, _ = z4.shape
    tq = min(DIFF_TQ, s)
    tk = min(DIFF_TK, s)
    assert s % tq == 0 and tq % tk == 0 and tk % LANES == 0
    tabs = _rope_tables_t(s, DIFF_HALF, tk)
    gt = lambda g: jnp.broadcast_to(jnp.concatenate([g, g])[:, None], (LANES, LANES))
    kv = lambda off: pl.BlockSpec((None, None, s, LANES), lambda bi, h, qi: (off + h, bi, 0, 0))
    tab = pl.BlockSpec((s // tk, 2 * DIFF_HALF, tk), lambda bi, h, qi: (0, 0, 0))
    sq = pl.BlockSpec((LANES, LANES), lambda bi, h, qi: (0, 0))
    vec = pl.BlockSpec((1, LANES), lambda bi, h, qi: (0, 0))
    return pl.pallas_call(
        functools.partial(_diff_kernel, seq=s, tq=tq, tk=tk, lam_init=lam_init),
        grid=(b, DIFF_HEADS, s // tq),
        in_specs=[
            kv(QC), kv(KC), kv(VC), tab, tab, sq, sq,
            pl.BlockSpec((4, DIFF_QK_DIM), lambda bi, h, qi: (0, 0)),
            vec,
        ],
        out_specs=pl.BlockSpec((None, None, tq, LANES), lambda bi, h, qi: (h, bi, qi, 0)),
        out_shape=jax.ShapeDtypeStruct((DIFF_HEADS, b, s, LANES), BF16),
        scratch_shapes=[pltpu.VMEM((s // tk, LANES, tk), BF16), pltpu.VMEM((s // tq, 2 * tq, LANES), BF16),
                        pltpu.SMEM((1,), F32)],
        compiler_params=_cparams(("parallel", "parallel", "arbitrary")),
        name="mixer_c",
    )(z4, z4, z4, *tabs, gt(qn), gt(kn), lamv, sub)


def _merge_kernel(x_ref, g_ref, sc_ref, sh_ref, oa_ref, ob_ref, oc_ref, wa_ref, wb_ref, wc_ref,
                  wga_ref, wgb_ref, wgc_ref, bga_ref, bgb_ref, bgc_ref, m_ref, h_scr):
    @pl.when(pl.program_id(1) == 0)
    def _():
        _modnorm_to(h_scr, x_ref, g_ref, sc_ref, sh_ref)

    h = h_scr[...]
    cat = lambda ref: jnp.concatenate([ref[i] for i in range(ref.shape[0])], axis=-1)

    def gate(w_ref, b_ref):
        t = jnp.dot(h, w_ref[...], preferred_element_type=F32) + b_ref[...]
        return 1.0 / (1.0 + jnp.exp(-t))

    y = gate(wga_ref, bga_ref) * jnp.dot(cat(oa_ref), wa_ref[...], preferred_element_type=F32)
    y = y + gate(wgb_ref, bgb_ref) * jnp.dot(cat(ob_ref), wb_ref[...], preferred_element_type=F32)
    y = y + gate(wgc_ref, bgc_ref) * jnp.dot(cat(oc_ref), wc_ref[...], preferred_element_type=F32)
    m_ref[...] = y.astype(BF16)


def _merge(x2, seq, g, sc, sh, oa, ob, oc, wa, wb, wc, wg, bg):
    t, d = x2.shape
    tm = _pick(seq, 1024)
    tn = _pick(d, 512)
    per = seq // tm
    nj = d // tn
    heads = lambda nh: pl.BlockSpec((nh, tm, LANES), lambda i, j: (0, i, 0))
    wcol = lambda k: pl.BlockSpec((k, tn), lambda i, j: (0, j))
    gcol = lambda sec: pl.BlockSpec((d, tn), lambda i, j: (0, sec * nj + j))
    bcol = lambda sec: pl.BlockSpec((1, tn), lambda i, j: (0, sec * nj + j))
    modspec = pl.BlockSpec((None, 1, d), lambda i, j: (i // per, 0, 0))
    bg2 = bg.reshape(1, 3 * d)
    return pl.pallas_call(
        _merge_kernel,
        grid=(t // tm, nj),
        in_specs=[
            pl.BlockSpec((tm, d), lambda i, j: (i, 0)), pl.BlockSpec((1, d), lambda i, j: (0, 0)),
            modspec, modspec,
            heads(NA_HEADS), heads(DIL_HEADS), heads(DIFF_HEADS),
            wcol(wa.shape[0]), wcol(wb.shape[0]), wcol(wc.shape[0]),
            gcol(0), gcol(1), gcol(2), bcol(0), bcol(1), bcol(2),
        ],
        out_specs=pl.BlockSpec((tm, tn), lambda i, j: (i, j)),
        out_shape=jax.ShapeDtypeStruct((t, d), BF16),
        scratch_shapes=[pltpu.VMEM((tm, d), BF16)],
        compiler_params=_cparams(("parallel", "arbitrary")),
        name="merge",
    )(x2, g, sc, sh, oa, ob, oc, wa, wb, wc, wg, wg, wg, bg2, bg2, bg2)


def _outproj_kernel(m_ref, w_ref, x_ref, g1_ref, o_ref):
    y = jnp.dot(m_ref[...], w_ref[...], preferred_element_type=F32)
    o_ref[...] = x_ref[...] + g1_ref[...] * y


def _outproj(m, w, x2, seq, g1):
    t, d = x2.shape
    tm = _pick(seq, 1024)
    tn = _pick(d, 1024)
    per = seq // tm
    return pl.pallas_call(
        _outproj_kernel,
        grid=(t // tm, d // tn),
        in_specs=[
            pl.BlockSpec((tm, d), lambda i, j: (i, 0)),
            pl.BlockSpec((d, tn), lambda i, j: (0, j)),
            pl.BlockSpec((tm, tn), lambda i, j: (i, j)),
            pl.BlockSpec((None, 1, tn), lambda i, j: (i // per, 0, j)),
        ],
        out_specs=pl.BlockSpec((tm, tn), lambda i, j: (i, j)),
        out_shape=jax.ShapeDtypeStruct((t, d), F32),
        compiler_params=_cparams(("parallel", "parallel")),
        name="out_proj",
    )(m, w, x2, g1)


MLP_ACC_COLS = 512


def _mlp_kernel(x_ref, g_ref, sc_ref, sh_ref, g2_ref, w1_ref, w2_ref, o_ref, h_scr):
    k = pl.program_id(1)

    @pl.when(k == 0)
    def _():
        _modnorm_to(h_scr, x_ref, g_ref, sc_ref, sh_ref)

        o_ref[...] = jnp.zeros_like(o_ref)

    u = jnp.maximum(jnp.dot(h_scr[...], w1_ref[...], preferred_element_type=F32), 0.0)
    u = (u * u).astype(BF16)
    d = o_ref.shape[1]
    nc = _pick(d, MLP_ACC_COLS)
    for c in range(d // nc):
        cs = slice(c * nc, (c + 1) * nc)
        o_ref[:, cs] += jnp.dot(u, w2_ref[:, cs], preferred_element_type=F32)

    @pl.when(k == pl.num_programs(1) - 1)
    def _():
        o_ref[...] = x_ref[...] + g2_ref[...] * o_ref[...]


def _mlp(x2, seq, g, sc, sh, g2, w1, w2):
    t, d = x2.shape
    ff = w1.shape[1]
    tm = _pick(seq, 1024)
    tf = _pick(ff, 512)
    per = seq // tm
    modspec = pl.BlockSpec((None, 1, d), lambda i, k: (i // per, 0, 0))
    return pl.pallas_call(
        _mlp_kernel,
        grid=(t // tm, ff // tf),
        in_specs=[
            pl.BlockSpec((tm, d), lambda i, k: (i, 0)),
            pl.BlockSpec((1, d), lambda i, k: (0, 0)),
            modspec, modspec, modspec,
            pl.BlockSpec((d, tf), lambda i, k: (0, k)),
            pl.BlockSpec((tf, d), lambda i, k: (k, 0)),
        ],
        out_specs=pl.BlockSpec((tm, d), lambda i, k: (i, 0)),
        out_shape=jax.ShapeDtypeStruct((t, d), F32),
        scratch_shapes=[pltpu.VMEM((tm, d), BF16)],
        compiler_params=_cparams(("parallel", "arbitrary")),
        name="mlp",
    )(x2, g, sc, sh, g2, w1, w2)


def _layer(x2, b, s, mods, layer_idx, p, tabs_b):
    d = x2.shape[1]
    sh1, sc1, g1, sh2, sc2, g2 = [m.reshape(b, 1, d) for m in jnp.split(mods, 6, axis=-1)]
    zm = _inproj(x2, s, p["norm1_g"], sc1, sh1, p["w_in_main"], BF16, "in_proj_main")
    zd = _inproj(x2, s, p["norm1_g"], sc1, sh1, p["w_in_dil"], F32, "in_proj_dil")
    zm4 = zm.reshape(zm.shape[0], b, s, LANES)
    zd4 = zd.reshape(zd.shape[0], b, s, LANES)
    oa = _na(zm4, p["bias_a"], p["qn_a"], p["kn_a"]).reshape(NA_HEADS, b * s, LANES)
    ob = _dil(zm4, zd4, tabs_b, p["qn_b"], p["kn_b"]).reshape(DIL_HEADS, b * s, LANES)
    lam_init = 0.8 - 0.6 * math.exp(-0.3 * layer_idx)
    oc = _diff(zm4, p["qn_c"], p["kn_c"], p["lamv"], p["subln_c"], lam_init)
    oc = oc.reshape(DIFF_HEADS, b * s, LANES)
    m = _merge(x2, s, p["norm1_g"], sc1, sh1, oa, ob, oc,
               p["w_br_a"], p["w_br_b"], p["w_br_c"], p["w_gate"], p["b_gate"])
    x2 = _outproj(m, p["w_out"], x2, s, g1)
    return _mlp(x2, s, p["norm2_g"], sc2, sh2, g2, p["w_ff1"], p["w_ff2"])


def kernel(x_prompt, x_sample, c_prompt, c_sample, norm1_g, w_ada, b_ada, w_in, qn_a, kn_a, rpb_a, qn_b, kn_b, qn_c, kn_c, lam_q1, lam_k1, lam_q2, lam_k2, subln_c, w_br_a, w_br_b, w_br_c, w_gate, b_gate, w_out, norm2_g, w_ff1, w_ff2):
    depth = w_in.shape[0]
    bp, sp, d = x_prompt.shape
    bs, ss, _ = x_sample.shape
    mods = _ada(jnp.concatenate([c_prompt, c_sample], axis=0), w_ada, b_ada)

    tabs = {s: _rope_tables(s, DIL_HALF, (0,)) for s in {sp, ss}}
    xp = x_prompt.reshape(bp * sp, d)
    xs = x_sample.reshape(bs * ss, d)
    for l in range(depth):
        w_in_main, w_in_dil = _split_w_in(w_in[l].astype(BF16))
        p = {
            "norm1_g": norm1_g[l].reshape(1, d), "norm2_g": norm2_g[l].reshape(1, d),
            "w_in_main": w_in_main, "w_in_dil": w_in_dil,
            "w_gate": w_gate[l].astype(BF16), "b_gate": b_gate[l],
            "w_br_a": w_br_a[l].astype(BF16), "w_br_b": w_br_b[l].astype(BF16),
            "w_br_c": w_br_c[l].astype(BF16), "w_out": w_out[l].astype(BF16),
            "w_ff1": w_ff1[l].astype(BF16), "w_ff2": w_ff2[l].astype(BF16),
            "qn_a": qn_a[l].reshape(1, LANES), "kn_a": kn_a[l].reshape(1, LANES),
            "qn_b": qn_b[l].reshape(1, LANES), "kn_b": kn_b[l].reshape(1, LANES),
            "qn_c": qn_c[l], "kn_c": kn_c[l],
            "lamv": jnp.stack([lam_q1[l], lam_k1[l], lam_q2[l], lam_k2[l]]).astype(F32),
            "subln_c": subln_c[l].reshape(1, LANES),
            "bias_a": _na_bias(rpb_a[l]),
        }
        xp = _layer(xp, bp, sp, mods[l, :bp], l, p, tabs[sp])
        xs = _layer(xs, bs, ss, mods[l, bp:], l, p, tabs[ss])
    return xp.reshape(bp, sp, d), xs.reshape(bs, ss, d)
```

```python
import functools
import math

import jax
import jax.numpy as jnp
import numpy as np
from jax import lax
from jax.experimental import pallas as pl
from jax.experimental.pallas import tpu as pltpu

F32 = jnp.float32
BF16 = jnp.bfloat16

LANES = 128
HEAD_DIM = 128
ROPE_THETA = 500000.0
EPS = 1e-6
NEG = -1e30
LOG2E = 1.4426950408889634
GRID_W = 64
WIN_ROWS = 8
WIN_COLS = 16
NA_HEADS = 4
DILATIONS = (1, 4, 16)
DIL_HEADS = 2
DIL_SPAN = 64
DIFF_HEADS = 6
DIFF_QK_DIM = 64
QA, KA, VA, B0, QC, KC, VC = 0, 4, 8, 12, 18, 24, 30

VMEM_LIMIT = 56 * 1024 * 1024


def _split_w_in(w):
    cols = lambda a, b: w[:, a * LANES:b * LANES]
    grp = lambda g: [cols(12 + 2 * g, 14 + 2 * g), cols(18 + 2 * g, 20 + 2 * g), cols(24 + 2 * g, 26 + 2 * g)]
    main = jnp.concatenate([cols(0, 12)] + grp(0) + [cols(30, 48)], axis=1)
    dil = jnp.concatenate(grp(1) + grp(2), axis=1)
    return main, dil


def _cparams(sem, vmem=VMEM_LIMIT):
    return pltpu.CompilerParams(dimension_semantics=sem, vmem_limit_bytes=vmem)


def _pick(n, pref):
    t = min(n, pref)
    while n % t:
        t //= 2
    return t


def _rms(x, width):
    return x * lax.rsqrt(jnp.sum(x * x, axis=-1, keepdims=True) * (1.0 / width) + EPS)


def _ada_kernel(c_ref, w_ref, b_ref, o_ref):
    c = c_ref[...]
    a = c * (1.0 / (1.0 + jnp.exp(-c)))
    o_ref[0] = jnp.dot(a.astype(BF16), w_ref[0].astype(BF16), preferred_element_type=F32) + b_ref[0]


def _ada(c_all, w_ada, b_ada):
    depth, d, n = w_ada.shape
    nb = c_all.shape[0]
    tn = _pick(n, 1024)
    return pl.pallas_call(
        _ada_kernel,
        grid=(depth, n // tn),
        in_specs=[
            pl.BlockSpec((nb, d), lambda l, j: (0, 0)),
            pl.BlockSpec((1, d, tn), lambda l, j: (l, 0, j)),
            pl.BlockSpec((1, 1, tn), lambda l, j: (l, 0, j)),
        ],
        out_specs=pl.BlockSpec((1, nb, tn), lambda l, j: (l, 0, j)),
        out_shape=jax.ShapeDtypeStruct((depth, nb, n), F32),
        compiler_params=_cparams(("parallel", "parallel")),
        name="ada_mod",
    )(c_all, w_ada, b_ada.reshape(depth, 1, n))


NORM_ROWS = 64


def _modnorm_to(h_scr, x_ref, g_ref, sc_ref, sh_ref):
    tm, d = x_ref.shape
    gs = g_ref[...] * (1.0 + sc_ref[...])
    sh = sh_ref[...]

    def body(i, carry):
        sl = pl.ds(pl.multiple_of(i * NORM_ROWS, NORM_ROWS), NORM_ROWS)
        x = x_ref[sl, :]
        h_scr[sl, :] = (_rms(x, d) * gs + sh).astype(BF16)
        return carry

    lax.fori_loop(0, tm // NORM_ROWS, body, 0, unroll=4)


def _inproj_kernel(x_ref, g_ref, sc_ref, sh_ref, w_ref, z_ref, h_scr):
    @pl.when(pl.program_id(1) == 0)
    def _():
        _modnorm_to(h_scr, x_ref, g_ref, sc_ref, sh_ref)

    acc = jnp.dot(h_scr[...], w_ref[...], preferred_element_type=F32)
    for c in range(z_ref.shape[0]):
        z_ref[c] = acc[:, c * LANES:(c + 1) * LANES].astype(z_ref.dtype)


def _inproj(x2, seq, g, sc, sh, w, out_dtype, name):
    t, d = x2.shape
    n = w.shape[1]
    tm = _pick(seq, 1024)
    tn = _pick(n, 1536)
    per = seq // tm
    return pl.pallas_call(
        _inproj_kernel,
        grid=(t // tm, n // tn),
        in_specs=[
            pl.BlockSpec((tm, d), lambda i, j: (i, 0)),
            pl.BlockSpec((1, d), lambda i, j: (0, 0)),
            pl.BlockSpec((None, 1, d), lambda i, j: (i // per, 0, 0)),
            pl.BlockSpec((None, 1, d), lambda i, j: (i // per, 0, 0)),
            pl.BlockSpec((d, tn), lambda i, j: (0, j)),
        ],
        out_specs=pl.BlockSpec((tn // LANES, tm, LANES), lambda i, j: (j, i, 0)),
        out_shape=jax.ShapeDtypeStruct((n // LANES, t, LANES), out_dtype),
        scratch_shapes=[pltpu.VMEM((tm, d), BF16)],
        compiler_params=_cparams(("parallel", "arbitrary")),
        name=name,
    )(x2, g, sc, sh, w)


def _na_bias(rpb):
    c = jnp.arange(GRID_W)
    cs = jnp.clip(c - WIN_COLS // 2, 0, GRID_W - WIN_COLS)
    kc = jnp.arange(GRID_W)
    inwin = (kc[None, :] >= cs[:, None]) & (kc[None, :] < cs[:, None] + WIN_COLS)
    dc = kc[None, :] - c[:, None] + WIN_COLS - 1
    oh_c = (dc[:, :, None] == jnp.arange(2 * WIN_COLS - 1)).astype(F32)
    dr = jnp.arange(WIN_ROWS)[None, :] - jnp.arange(WIN_ROWS)[:, None] + WIN_ROWS - 1
    oh_r = (dr[:, :, None] == jnp.arange(2 * WIN_ROWS - 1)).astype(F32)
    b = jnp.einsum("hij,dri,ckj->hdcrk", rpb.astype(F32), oh_r, oh_c, precision=lax.Precision.HIGHEST)
    b = jnp.where(inwin[None, None, :, None, :], b, NEG)
    return b.reshape(rpb.shape[0], WIN_ROWS, GRID_W, WIN_ROWS * GRID_W)


NA_GROUP = 8


def _na_kernel(q_ref, k_ref, v_ref, bias_ref, qn_ref, kn_ref, o_ref, qn_scr, kn_scr, *, rows):
    band = WIN_ROWS * GRID_W
    kg = kn_ref[...]
    qg = qn_ref[...] * HEAD_DIM ** -0.5

    def norm(i, carry):
        sl = pl.ds(pl.multiple_of(i * 256, 256), 256)
        kn_scr[sl, :] = (_rms(k_ref[sl, :].astype(F32), HEAD_DIM) * kg).astype(BF16)
        qn_scr[sl, :] = (_rms(q_ref[sl, :].astype(F32), HEAD_DIM) * qg).astype(BF16)
        return carry

    lax.fori_loop(0, rows * GRID_W // 256, norm, 0, unroll=2)

    def body(g, carry):
        rr = [g * NA_GROUP + i for i in range(NA_GROUP)]
        rs = [jnp.clip(r - WIN_ROWS // 2, 0, rows - WIN_ROWS) for r in rr]
        qsl = [pl.ds(pl.multiple_of(r * GRID_W, GRID_W), GRID_W) for r in rr]
        ksl = [pl.ds(pl.multiple_of(r0 * GRID_W, GRID_W), band) for r0 in rs]
        s = [lax.dot_general(qn_scr[qsl[i], :], kn_scr[ksl[i], :], (((1,), (1,)), ((), ())),
                             preferred_element_type=F32) for i in range(NA_GROUP)]
        p, l = [], []
        for i in range(NA_GROUP):
            si = s[i] + bias_ref[rr[i] - rs[i]]
            pi = jnp.exp(si - jnp.max(si, axis=-1, keepdims=True))
            l.append(jnp.sum(pi, axis=-1, keepdims=True))
            p.append(pi.astype(BF16))
        for i in range(NA_GROUP):
            o = jnp.dot(p[i], v_ref[ksl[i], :], preferred_element_type=F32) / l[i]
            o_ref[qsl[i], :] = o.astype(BF16)
        return carry

    lax.fori_loop(0, rows // NA_GROUP, body, 0)


def _na(z4, bias, qn, kn):
    _, b, s, _ = z4.shape
    rows = s // GRID_W
    assert rows >= WIN_ROWS and rows % NA_GROUP == 0 and s % 256 == 0
    slab = lambda off: pl.BlockSpec((None, None, s, LANES), lambda bi, h: (off + h, bi, 0, 0))
    return pl.pallas_call(
        functools.partial(_na_kernel, rows=rows),
        grid=(b, NA_HEADS),
        in_specs=[
            slab(QA), slab(KA), slab(VA),
            pl.BlockSpec((None, WIN_ROWS, GRID_W, WIN_ROWS * GRID_W), lambda bi, h: (h, 0, 0, 0)),
            pl.BlockSpec((1, LANES), lambda bi, h: (0, 0)),
            pl.BlockSpec((1, LANES), lambda bi, h: (0, 0)),
        ],
        out_specs=pl.BlockSpec((None, None, s, LANES), lambda bi, h: (h, bi, 0, 0)),
        out_shape=jax.ShapeDtypeStruct((NA_HEADS, b, s, LANES), BF16),
        scratch_shapes=[pltpu.VMEM((s, LANES), BF16), pltpu.VMEM((s, LANES), BF16)],
        compiler_params=_cparams(("parallel", "parallel")),
        name="mixer_a",
    )(z4, z4, z4, bias, qn, kn)


def _rope_cos_sin(seq, half):
    inv = (np.float32(ROPE_THETA) ** (-np.arange(half, dtype=np.float32) / np.float32(half))).astype(np.float32)
    ang = (np.arange(seq, dtype=np.float32)[:, None] * inv[None, :]).astype(np.float64)
    return np.cos(ang).astype(np.float32), np.sin(ang).astype(np.float32)


def _rope_tables_t(seq, half, chunk, order=None):
    cos, sin = _rope_cos_sin(seq, half)
    if order is not None:
        cos, sin = cos[order], sin[order]
    c = np.concatenate([cos, cos], axis=1).T
    sn = np.concatenate([-sin, sin], axis=1).T
    split = lambda t: jnp.asarray(t.reshape(2 * half, seq // chunk, chunk).transpose(1, 0, 2))
    return split(c), split(sn)


def _norm_rope_t(x, gt, ct, snt, comp, half):
    rows = x.shape[0]
    xt = x.T
    g = gt[:, :rows] if rows < LANES else jnp.concatenate([gt] * (rows // LANES), axis=1)
    w = 2 * half

    def one(y, gc):
        r = lax.rsqrt(jnp.sum(y * y, axis=0, keepdims=True) * (1.0 / comp) + EPS)
        y = y * r * gc
        head = y[:w]
        rot = jnp.concatenate([head[half:], head[:half]], axis=0)
        return jnp.concatenate([head * ct + rot * snt, y[w:]], axis=0)

    parts = [one(xt[o:o + comp], g[o:o + comp]) for o in range(0, LANES, comp)]
    return parts[0] if len(parts) == 1 else jnp.concatenate(parts, axis=0)


def _lane_bcast(g):
    return jnp.broadcast_to(g[:, None], (LANES, LANES))


DIL_HALF = HEAD_DIM // 4 // 2
DIL_GROUP = 4


def _dil_kernel(q0_ref, k0_ref, v0_ref, q1_ref, k1_ref, v1_ref, q2_ref, k2_ref, v2_ref,
                ct0, st0, ct1, st1, ct2, st2, qgt_ref, kgt_ref, o_ref,
                qs0, ks0, vs0, qs1, ks1, vs1, qs2, ks2, vs2, og0, og1, og2, ls0, ls1, ls2, *, seq):
    kg = kgt_ref[...]
    qg = qgt_ref[...] * HEAD_DIM ** -0.5
    groups = ((q0_ref, k0_ref, v0_ref, ct0, st0, qs0, ks0, vs0, og0, ls0),
              (q1_ref, k1_ref, v1_ref, ct1, st1, qs1, ks1, vs1, og1, ls1),
              (q2_ref, k2_ref, v2_ref, ct2, st2, qs2, ks2, vs2, og2, ls2))
    preps, attns = [], []

    for (qr, kr, vr, ct, st, qn_scr, kn_scr, v_scr, og, ls), d in zip(groups, DILATIONS):
        n = seq // d
        tq = min(128, n)
        kw = min(2 * tq, n)
        ck = min(256, n)
        nb = n // tq
        nc = n // ck

        def cls(start, size, d=d):
            return pl.ds(start, size) if d == 1 else pl.ds(start, size, stride=d)

        def prep(u, d=d, ck=ck, nc=nc, qr=qr, kr=kr, vr=vr, ct=ct, st=st,
                 qn_scr=qn_scr, kn_scr=kn_scr, v_scr=v_scr, cls=cls):
            start = u // nc + d * ck * (u % nc)
            if d == 1:
                start = pl.multiple_of(start, ck)
            src = cls(start, ck)
            dst = pl.ds(pl.multiple_of(u * ck, ck), ck)
            c, sn = ct[u], st[u]
            k = _norm_rope_t(kr[src, :].astype(F32), kg, c, sn, HEAD_DIM, DIL_HALF)
            kn_scr[dst, :] = k.T.astype(BF16)
            q = _norm_rope_t(qr[src, :].astype(F32), qg, c, sn, HEAD_DIM, DIL_HALF)
            qn_scr[dst, :] = q.T.astype(BF16)
            v_scr[dst, :] = vr[src, :].astype(BF16)

        def qblocks(gi, d=d, n=n, tq=tq, kw=kw, nb=nb, qn_scr=qn_scr, kn_scr=kn_scr, v_scr=v_scr,
                    og=og, ls=ls, cls=cls):
            uu = [gi * DIL_GROUP + i for i in range(DIL_GROUP)]
            q0 = [(u % nb) * tq for u in uu]
            ks = [jnp.clip(q - DIL_SPAN, 0, n - kw) for q in q0]
            ksl = [pl.ds(pl.multiple_of((u // nb) * n + k0, DIL_SPAN), kw) for u, k0 in zip(uu, ks)]
            s = [lax.dot_general(qn_scr[pl.ds(pl.multiple_of(u * tq, tq), tq), :], kn_scr[ksl[i], :],
                                 (((1,), (1,)), ((), ())), preferred_element_type=F32)
                 for i, u in enumerate(uu)]
            off = (lax.broadcasted_iota(jnp.int32, (tq, kw), 1) - lax.broadcasted_iota(jnp.int32, (tq, kw), 0))
            p, l, lse = [], [], []
            for i in range(DIL_GROUP):
                si = jnp.where(jnp.abs(off + (ks[i] - q0[i])) <= DIL_SPAN, s[i], NEG)
                m = jnp.max(si, axis=-1, keepdims=True)
                pi = jnp.exp(si - m)
                li = jnp.sum(pi, axis=-1, keepdims=True)
                p.append(pi.astype(BF16))
                l.append(li)
                lse.append(m + jnp.log(li))
            for i, u in enumerate(uu):
                start = u // nb + d * q0[i]
                if d == 1:
                    start = pl.multiple_of(start, tq)
                o = jnp.dot(p[i], v_scr[ksl[i], :], preferred_element_type=F32) / l[i]
                og[cls(start, tq), :] = o
                ls[cls(start, tq), :] = jnp.broadcast_to(lse[i], (tq, LANES))

        preps.append((prep, seq // ck))
        attns.append((qblocks, seq // tq // DIL_GROUP))

    def stage(prep, attn):
        if attn is None:
            fn, trips = prep
            return lax.fori_loop(0, trips, lambda u, c: (fn(u), c)[1], 0, unroll=2)
        afn, trips = attn
        if prep is None:
            return lax.fori_loop(0, trips, lambda g, c: (afn(g), c)[1], 0)
        pfn, chunks = prep
        assert chunks % trips == 0
        per = chunks // trips

        def body(g, carry):
            for i in range(per):
                pfn(g * per + i)
            afn(g)
            return carry

        return lax.fori_loop(0, trips, body, 0)

    stage(preps[0], None)
    stage(preps[1], attns[0])
    stage(preps[2], attns[1])
    stage(None, attns[2])

    def mix(i, carry):
        sl = pl.ds(pl.multiple_of(i * 256, 256), 256)
        l0, l1, l2 = ls0[sl, :], ls1[sl, :], ls2[sl, :]
        mx = jnp.maximum(jnp.maximum(l0, l1), l2)
        e0, e1, e2 = jnp.exp(l0 - mx), jnp.exp(l1 - mx), jnp.exp(l2 - mx)
        o = (e0 * og0[sl, :] + e1 * og1[sl, :] + e2 * og2[sl, :]) / (e0 + e1 + e2)
        o_ref[sl, :] = o.astype(BF16)
        return carry

    lax.fori_loop(0, seq // 256, mix, 0)


def _dil(zm4, zd4, qn, kn):
    _, b, s, _ = zm4.shape
    assert s % 256 == 0 and s // DILATIONS[-1] >= DIL_SPAN and (s // 128) % DIL_GROUP == 0
    once = dict(pipeline_mode=pl.Buffered(1))
    main = lambda off: pl.BlockSpec((None, None, s, LANES), lambda bi, hh: (off + hh, bi, 0, 0))
    dil = lambda off: pl.BlockSpec((None, None, s, LANES), lambda bi, hh: (off + hh, bi, 0, 0), **once)
    sq = pl.BlockSpec((LANES, LANES), lambda bi, hh: (0, 0))
    tabs, tab_specs = [], []
    for d in DILATIONS:
        n = s // d
        ck = min(256, n)
        flat = np.arange(s)
        tabs += list(_rope_tables_t(s, DIL_HALF, ck, order=(flat % n) * d + flat // n))
        tab_specs += [pl.BlockSpec((s // ck, 2 * DIL_HALF, ck), lambda bi, hh: (0, 0, 0), **once)] * 2
    return pl.pallas_call(
        functools.partial(_dil_kernel, seq=s),
        grid=(b, DIL_HEADS),
        in_specs=[main(B0), main(B0 + 2), main(B0 + 4),
                  dil(0), dil(2), dil(4), dil(6), dil(8), dil(10),
                  *tab_specs, sq, sq],
        out_specs=pl.BlockSpec((None, None, s, LANES), lambda bi, hh: (hh, bi, 0, 0)),
        out_shape=jax.ShapeDtypeStruct((DIL_HEADS, b, s, LANES), BF16),
        scratch_shapes=[pltpu.VMEM((s, LANES), BF16)] * 9 + [pltpu.VMEM((s, LANES), F32)] * 6,
        compiler_params=_cparams(("parallel", "parallel")),
        name="mixer_b",
    )(zm4, zm4, zm4, zd4, zd4, zd4, zd4, zd4, zd4, *tabs, _lane_bcast(qn), _lane_bcast(kn))


DIFF_HALF = DIFF_QK_DIM // 4 // 2
DIFF_TQ = 512
DIFF_TK = 256
DIFF_SAFE_BOUND = 50.0


def _diff_kernel(q_ref, k_ref, v_ref, ct_ref, snt_ref, qgt_ref, kgt_ref, lam_ref, sub_ref, o_ref,
                 kt_scr, q2_scr, bound_scr, *, seq, tq, tk, lam_init):
    @pl.when(pl.program_id(2) == 0)
    def _():
        kg = kgt_ref[...]
        qg = qgt_ref[...] * (DIFF_QK_DIM ** -0.5 * LOG2E)

        def knorm(i, carry):
            sl = pl.ds(pl.multiple_of(i * tk, tk), tk)
            kt_scr[i] = _norm_rope_t(k_ref[sl, :].astype(F32), kg, ct_ref[i], snt_ref[i],
                                     DIFF_QK_DIM, DIFF_HALF).astype(BF16)
            return carry

        lax.fori_loop(0, seq // tk, knorm, 0, unroll=4)

        per = tq // tk

        def qnorm(i, carry):
            sl = pl.ds(pl.multiple_of(i * tk, tk), tk)
            q = _norm_rope_t(q_ref[sl, :].astype(F32), qg, ct_ref[i], snt_ref[i], DIFF_QK_DIM, DIFF_HALF).T
            lo = lax.broadcasted_iota(jnp.int32, q.shape, 1) < DIFF_QK_DIM
            blk = i // per
            off = (i % per) * tk
            q2_scr[blk, pl.ds(pl.multiple_of(off, tk), tk), :] = jnp.where(lo, q, 0.0).astype(BF16)
            q2_scr[blk, pl.ds(pl.multiple_of(tq + off, tk), tk), :] = jnp.where(lo, 0.0, q).astype(BF16)
            return carry

        lax.fori_loop(0, seq // tk, qnorm, 0, unroll=4)

        gmax = jnp.max(jnp.abs(qg)) * jnp.max(jnp.abs(kg))
        bound_scr[0] = gmax * (DIFF_QK_DIM * 1.01)

    nk = seq // tk
    lane_sums = lambda p: functools.reduce(jnp.add, [p[:, c * LANES:(c + 1) * LANES] for c in range(tk // LANES)])

    def finish(acc, l):
        lv = lam_ref[...]
        lam = (jnp.exp(jnp.sum(lv[0:1] * lv[1:2], axis=-1, keepdims=True))
               - jnp.exp(jnp.sum(lv[2:3] * lv[3:4], axis=-1, keepdims=True)) + lam_init)
        on = acc / jnp.sum(l, axis=-1, keepdims=True)
        o = on[:tq] - lam * on[tq:]
        o_ref[...] = (_rms(o, HEAD_DIM) * sub_ref[...] * (1.0 - lam_init)).astype(BF16)

    def attend(shift):
        q2 = q2_scr[pl.program_id(2)]
        m = jnp.full((2 * tq, 1), -jnp.inf, F32)
        l = jnp.zeros((2 * tq, LANES), F32)
        acc = jnp.zeros((2 * tq, HEAD_DIM), F32)
        s_next = jnp.dot(q2, kt_scr[0], preferred_element_type=F32)
        for j in range(nk):
            s = s_next
            if j + 1 < nk:
                s_next = jnp.dot(q2, kt_scr[j + 1], preferred_element_type=F32)
            v = v_ref[j * tk:(j + 1) * tk, :]
            if shift is None:
                m_new = jnp.maximum(m, jnp.max(s, axis=-1, keepdims=True))
                alpha = jnp.exp2(m - m_new)
                p = jnp.exp2(s - m_new)
                l = alpha * l + lane_sums(p)
                acc = alpha * acc + jnp.dot(p.astype(BF16), v, preferred_element_type=F32)
                m = m_new
            else:
                p = jnp.exp2(s - shift)
                l = l + lane_sums(p)
                acc = acc + jnp.dot(p.astype(BF16), v, preferred_element_type=F32)
        finish(acc, l)

    bound = bound_scr[0]
    safe = bound <= DIFF_SAFE_BOUND

    @pl.when(safe)
    def _():
        attend(bound)

    @pl.when(jnp.logical_not(safe))
    def _():
        attend(None)


def _diff(z4, qn, kn, lamv, sub, lam_init):
    _, b, s, _ = z4.shape
    tq = min(DIFF_TQ, s)
    tk = min(DIFF_TK, s)
    assert s % tq == 0 and tq % tk == 0 and tk % LANES == 0
    tabs = _rope_tables_t(s, DIFF_HALF, tk)
    gt = lambda g: _lane_bcast(jnp.concatenate([g, g]))
    kv = lambda off: pl.BlockSpec((None, None, s, LANES), lambda bi, h, qi: (off + h, bi, 0, 0))
    tab = pl.BlockSpec((s // tk, 2 * DIFF_HALF, tk), lambda bi, h, qi: (0, 0, 0))
    sq = pl.BlockSpec((LANES, LANES), lambda bi, h, qi: (0, 0))
    vec = pl.BlockSpec((1, LANES), lambda bi, h, qi: (0, 0))
    return pl.pallas_call(
        functools.partial(_diff_kernel, seq=s, tq=tq, tk=tk, lam_init=lam_init),
        grid=(b, DIFF_HEADS, s // tq),
        in_specs=[
            kv(QC), kv(KC), kv(VC), tab, tab, sq, sq,
            pl.BlockSpec((4, DIFF_QK_DIM), lambda bi, h, qi: (0, 0)),
            vec,
        ],
        out_specs=pl.BlockSpec((None, None, tq, LANES), lambda bi, h, qi: (h, bi, qi, 0)),
        out_shape=jax.ShapeDtypeStruct((DIFF_HEADS, b, s, LANES), BF16),
        scratch_shapes=[pltpu.VMEM((s // tk, LANES, tk), BF16), pltpu.VMEM((s // tq, 2 * tq, LANES), BF16),
                        pltpu.SMEM((1,), F32)],
        compiler_params=_cparams(("parallel", "parallel", "arbitrary")),
        name="mixer_c",
    )(z4, z4, z4, *tabs, gt(qn), gt(kn), lamv, sub)


def _merge_kernel(x_ref, g_ref, sc_ref, sh_ref, oa_ref, ob_ref, oc_ref, wa_ref, wb_ref, wc_ref,
                  wga_ref, wgb_ref, wgc_ref, bga_ref, bgb_ref, bgc_ref, m_ref, h_scr):
    @pl.when(pl.program_id(1) == 0)
    def _():
        _modnorm_to(h_scr, x_ref, g_ref, sc_ref, sh_ref)

    h = h_scr[...]
    cat = lambda ref: jnp.concatenate([ref[i] for i in range(ref.shape[0])], axis=-1)

    def gate(w_ref, b_ref):
        t = jnp.dot(h, w_ref[...], preferred_element_type=F32) + b_ref[...]
        return 1.0 / (1.0 + jnp.exp(-t))

    y = gate(wga_ref, bga_ref) * jnp.dot(cat(oa_ref), wa_ref[...], preferred_element_type=F32)
    y = y + gate(wgb_ref, bgb_ref) * jnp.dot(cat(ob_ref), wb_ref[...], preferred_element_type=F32)
    y = y + gate(wgc_ref, bgc_ref) * jnp.dot(cat(oc_ref), wc_ref[...], preferred_element_type=F32)
    m_ref[...] = y.astype(BF16)


def _merge(x2, seq, g, sc, sh, oa, ob, oc, wa, wb, wc, wg, bg):
    t, d = x2.shape
    tm = _pick(seq, 1024)
    tn = _pick(d, 512)
    per = seq // tm
    nj = d // tn
    heads = lambda nh: pl.BlockSpec((nh, tm, LANES), lambda i, j: (0, i, 0))
    wcol = lambda k: pl.BlockSpec((k, tn), lambda i, j: (0, j))
    gcol = lambda sec: pl.BlockSpec((d, tn), lambda i, j: (0, sec * nj + j))
    bcol = lambda sec: pl.BlockSpec((1, tn), lambda i, j: (0, sec * nj + j))
    modspec = pl.BlockSpec((None, 1, d), lambda i, j: (i // per, 0, 0))
    bg2 = bg.reshape(1, 3 * d)
    return pl.pallas_call(
        _merge_kernel,
        grid=(t // tm, nj),
        in_specs=[
            pl.BlockSpec((tm, d), lambda i, j: (i, 0)), pl.BlockSpec((1, d), lambda i, j: (0, 0)),
            modspec, modspec,
            heads(NA_HEADS), heads(DIL_HEADS), heads(DIFF_HEADS),
            wcol(wa.shape[0]), wcol(wb.shape[0]), wcol(wc.shape[0]),
            gcol(0), gcol(1), gcol(2), bcol(0), bcol(1), bcol(2),
        ],
        out_specs=pl.BlockSpec((tm, tn), lambda i, j: (i, j)),
        out_shape=jax.ShapeDtypeStruct((t, d), BF16),
        scratch_shapes=[pltpu.VMEM((tm, d), BF16)],
        compiler_params=_cparams(("parallel", "arbitrary")),
        name="merge",
    )(x2, g, sc, sh, oa, ob, oc, wa, wb, wc, wg, wg, wg, bg2, bg2, bg2)


def _outproj_kernel(m_ref, w_ref, x_ref, g1_ref, o_ref):
    y = jnp.dot(m_ref[...], w_ref[...], preferred_element_type=F32)
    o_ref[...] = x_ref[...] + g1_ref[...] * y


def _outproj(m, w, x2, seq, g1):
    t, d = x2.shape
    tm = _pick(seq, 1024)
    tn = _pick(d, 1024)
    per = seq // tm
    return pl.pallas_call(
        _outproj_kernel,
        grid=(t // tm, d // tn),
        in_specs=[
            pl.BlockSpec((tm, d), lambda i, j: (i, 0)),
            pl.BlockSpec((d, tn), lambda i, j: (0, j)),
            pl.BlockSpec((tm, tn), lambda i, j: (i, j)),
            pl.BlockSpec((None, 1, tn), lambda i, j: (i // per, 0, j)),
        ],
        out_specs=pl.BlockSpec((tm, tn), lambda i, j: (i, j)),
        out_shape=jax.ShapeDtypeStruct((t, d), F32),
        compiler_params=_cparams(("parallel", "parallel")),
        name="out_proj",
    )(m, w, x2, g1)


MLP_ACC_COLS = 512


def _mlp_kernel(x_ref, g_ref, sc_ref, sh_ref, g2_ref, w1_ref, w2_ref, o_ref, h_scr):
    k = pl.program_id(1)
    last = pl.num_programs(1) - 1
    d = o_ref.shape[1]
    nc = _pick(d, MLP_ACC_COLS)

    def step(first, final):
        u = jnp.maximum(jnp.dot(h_scr[...], w1_ref[...], preferred_element_type=F32), 0.0)
        u = (u * u).astype(BF16)
        for c in range(d // nc):
            cs = slice(c * nc, (c + 1) * nc)
            f = jnp.dot(u, w2_ref[:, cs], preferred_element_type=F32)
            if not first:
                f = o_ref[:, cs] + f
            if final:
                f = x_ref[:, cs] + g2_ref[:, cs] * f
            o_ref[:, cs] = f

    @pl.when(k == 0)
    def _():
        _modnorm_to(h_scr, x_ref, g_ref, sc_ref, sh_ref)
        step(True, False)

    @pl.when(jnp.logical_and(k > 0, k < last))
    def _():
        step(False, False)

    @pl.when(k == last)
    def _():
        step(False, True)


def _mlp(x2, seq, g, sc, sh, g2, w1, w2):
    t, d = x2.shape
    ff = w1.shape[1]
    tm = _pick(seq, 1024)
    tf = _pick(ff, 512)
    assert ff // tf >= 2
    per = seq // tm
    modspec = pl.BlockSpec((None, 1, d), lambda i, k: (i // per, 0, 0))
    return pl.pallas_call(
        _mlp_kernel,
        grid=(t // tm, ff // tf),
        in_specs=[
            pl.BlockSpec((tm, d), lambda i, k: (i, 0)),
            pl.BlockSpec((1, d), lambda i, k: (0, 0)),
            modspec, modspec, modspec,
            pl.BlockSpec((d, tf), lambda i, k: (0, k)),
            pl.BlockSpec((tf, d), lambda i, k: (k, 0)),
        ],
        out_specs=pl.BlockSpec((tm, d), lambda i, k: (i, 0)),
        out_shape=jax.ShapeDtypeStruct((t, d), F32),
        scratch_shapes=[pltpu.VMEM((tm, d), BF16)],
        compiler_params=_cparams(("parallel", "arbitrary")),
        name="mlp",
    )(x2, g, sc, sh, g2, w1, w2)


def _layer(x2, b, s, mods, layer_idx, p):
    d = x2.shape[1]
    sh1, sc1, g1, sh2, sc2, g2 = [m.reshape(b, 1, d) for m in jnp.split(mods, 6, axis=-1)]
    zm = _inproj(x2, s, p["norm1_g"], sc1, sh1, p["w_in_main"], BF16, "in_proj_main")
    zd = _inproj(x2, s, p["norm1_g"], sc1, sh1, p["w_in_dil"], F32, "in_proj_dil")
    zm4 = zm.reshape(zm.shape[0], b, s, LANES)
    zd4 = zd.reshape(zd.shape[0], b, s, LANES)
    oa = _na(zm4, p["bias_a"], p["qn_a"], p["kn_a"]).reshape(NA_HEADS, b * s, LANES)
    ob = _dil(zm4, zd4, p["qn_b"], p["kn_b"]).reshape(DIL_HEADS, b * s, LANES)
    lam_init = 0.8 - 0.6 * math.exp(-0.3 * layer_idx)
    oc = _diff(zm4, p["qn_c"], p["kn_c"], p["lamv"], p["subln_c"], lam_init)
    oc = oc.reshape(DIFF_HEADS, b * s, LANES)
    m = _merge(x2, s, p["norm1_g"], sc1, sh1, oa, ob, oc,
               p["w_br_a"], p["w_br_b"], p["w_br_c"], p["w_gate"], p["b_gate"])
    x2 = _outproj(m, p["w_out"], x2, s, g1)
    return _mlp(x2, s, p["norm2_g"], sc2, sh2, g2, p["w_ff1"], p["w_ff2"])


def kernel(x_prompt, x_sample, c_prompt, c_sample, norm1_g, w_ada, b_ada, w_in, qn_a, kn_a, rpb_a, qn_b, kn_b, qn_c, kn_c, lam_q1, lam_k1, lam_q2, lam_k2, subln_c, w_br_a, w_br_b, w_br_c, w_gate, b_gate, w_out, norm2_g, w_ff1, w_ff2):
    depth = w_in.shape[0]
    bp, sp, d = x_prompt.shape
    bs, ss, _ = x_sample.shape
    mods = _ada(jnp.concatenate([c_prompt, c_sample], axis=0), w_ada, b_ada)

    xp = x_prompt.reshape(bp * sp, d)
    xs = x_sample.reshape(bs * ss, d)
    for l in range(depth):
        w_in_main, w_in_dil = _split_w_in(w_in[l].astype(BF16))
        p = {
            "norm1_g": norm1_g[l].reshape(1, d), "norm2_g": norm2_g[l].reshape(1, d),
            "w_in_main": w_in_main, "w_in_dil": w_in_dil,
            "w_gate": w_gate[l].astype(BF16), "b_gate": b_gate[l],
            "w_br_a": w_br_a[l].astype(BF16), "w_br_b": w_br_b[l].astype(BF16),
            "w_br_c": w_br_c[l].astype(BF16), "w_out": w_out[l].astype(BF16),
            "w_ff1": w_ff1[l].astype(BF16), "w_ff2": w_ff2[l].astype(BF16),
            "qn_a": qn_a[l].reshape(1, LANES), "kn_a": kn_a[l].reshape(1, LANES),
            "qn_b": qn_b[l], "kn_b": kn_b[l],
            "qn_c": qn_c[l], "kn_c": kn_c[l],
            "lamv": jnp.stack([lam_q1[l], lam_k1[l], lam_q2[l], lam_k2[l]]).astype(F32),
            "subln_c": subln_c[l].reshape(1, LANES),
            "bias_a": _na_bias(rpb_a[l]),
        }
        xp = _layer(xp, bp, sp, mods[l, :bp], l, p)
        xs = _layer(xs, bs, ss, mods[l, bp:], l, p)
    return xp.reshape(bp, sp, d), xs.reshape(bs, ss, d)
```

```python
import functools
import math

import jax
import jax.numpy as jnp
import numpy as np
from jax import lax
from jax.experimental import pallas as pl
from jax.experimental.pallas import tpu as pltpu

F32 = jnp.float32
BF16 = jnp.bfloat16

LANES = 128
HEAD_DIM = 128
ROPE_THETA = 500000.0
EPS = 1e-6
NEG = -1e30
LOG2E = 1.4426950408889634
GRID_W = 64
WIN_ROWS = 8
WIN_COLS = 16
NA_HEADS = 4
DILATIONS = (1, 4, 16)
DIL_HEADS = 2
DIL_SPAN = 64
DIFF_HEADS = 6
DIFF_QK_DIM = 64
QA, KA, VA, B0, QC, KC, VC = 0, 4, 8, 12, 18, 24, 30

VMEM_LIMIT = 56 * 1024 * 1024


def _split_w_in(w):
    cols = lambda a, b: w[:, a * LANES:b * LANES]
    grp = lambda g: [cols(12 + 2 * g, 14 + 2 * g), cols(18 + 2 * g, 20 + 2 * g), cols(24 + 2 * g, 26 + 2 * g)]
    main = jnp.concatenate([cols(0, 12)] + grp(0) + [cols(30, 48)], axis=1)
    dil = jnp.concatenate(grp(1) + grp(2), axis=1)
    return main, dil


def _cparams(sem, vmem=VMEM_LIMIT):
    return pltpu.CompilerParams(dimension_semantics=sem, vmem_limit_bytes=vmem)


def _pick(n, pref):
    t = min(n, pref)
    while n % t:
        t //= 2
    return t


def _rms(x, width):
    return x * lax.rsqrt(jnp.sum(x * x, axis=-1, keepdims=True) * (1.0 / width) + EPS)


def _ada_kernel(c_ref, w_ref, b_ref, o_ref):
    c = c_ref[...]
    a = c * (1.0 / (1.0 + jnp.exp(-c)))
    o_ref[0] = jnp.dot(a.astype(BF16), w_ref[0].astype(BF16), preferred_element_type=F32) + b_ref[0]


def _ada(c_all, w_ada, b_ada):
    depth, d, n = w_ada.shape
    nb = c_all.shape[0]
    tn = _pick(n, 1024)
    return pl.pallas_call(
        _ada_kernel,
        grid=(depth, n // tn),
        in_specs=[
            pl.BlockSpec((nb, d), lambda l, j: (0, 0)),
            pl.BlockSpec((1, d, tn), lambda l, j: (l, 0, j)),
            pl.BlockSpec((1, 1, tn), lambda l, j: (l, 0, j)),
        ],
        out_specs=pl.BlockSpec((1, nb, tn), lambda l, j: (l, 0, j)),
        out_shape=jax.ShapeDtypeStruct((depth, nb, n), F32),
        compiler_params=_cparams(("parallel", "parallel")),
        name="ada_mod",
    )(c_all, w_ada, b_ada.reshape(depth, 1, n))


NORM_ROWS = 64


def _modnorm_to(h_scr, x_ref, g_ref, sc_ref, sh_ref):
    tm, d = x_ref.shape
    gs = g_ref[...] * (1.0 + sc_ref[...])
    sh = sh_ref[...]

    def body(i, carry):
        sl = pl.ds(pl.multiple_of(i * NORM_ROWS, NORM_ROWS), NORM_ROWS)
        x = x_ref[sl, :]
        h_scr[sl, :] = (_rms(x, d) * gs + sh).astype(BF16)
        return carry

    lax.fori_loop(0, tm // NORM_ROWS, body, 0, unroll=4)


def _inproj_kernel(x_ref, g_ref, sc_ref, sh_ref, w_ref, z_ref, h_scr):
    @pl.when(pl.program_id(1) == 0)
    def _():
        _modnorm_to(h_scr, x_ref, g_ref, sc_ref, sh_ref)

    acc = jnp.dot(h_scr[...], w_ref[...], preferred_element_type=F32)
    for c in range(z_ref.shape[0]):
        z_ref[c] = acc[:, c * LANES:(c + 1) * LANES].astype(z_ref.dtype)


def _inproj(x2, seq, g, sc, sh, w, out_dtype, name):
    t, d = x2.shape
    n = w.shape[1]
    tm = _pick(seq, 1024)
    tn = _pick(n, 1536)
    per = seq // tm
    return pl.pallas_call(
        _inproj_kernel,
        grid=(t // tm, n // tn),
        in_specs=[
            pl.BlockSpec((tm, d), lambda i, j: (i, 0)),
            pl.BlockSpec((1, d), lambda i, j: (0, 0)),
            pl.BlockSpec((None, 1, d), lambda i, j: (i // per, 0, 0)),
            pl.BlockSpec((None, 1, d), lambda i, j: (i // per, 0, 0)),
            pl.BlockSpec((d, tn), lambda i, j: (0, j)),
        ],
        out_specs=pl.BlockSpec((tn // LANES, tm, LANES), lambda i, j: (j, i, 0)),
        out_shape=jax.ShapeDtypeStruct((n // LANES, t, LANES), out_dtype),
        scratch_shapes=[pltpu.VMEM((tm, d), BF16)],
        compiler_params=_cparams(("parallel", "arbitrary")),
        name=name,
    )(x2, g, sc, sh, w)


def _na_bias(rpb):
    c = jnp.arange(GRID_W)
    cs = jnp.clip(c - WIN_COLS // 2, 0, GRID_W - WIN_COLS)
    kc = jnp.arange(GRID_W)
    inwin = (kc[None, :] >= cs[:, None]) & (kc[None, :] < cs[:, None] + WIN_COLS)
    dc = kc[None, :] - c[:, None] + WIN_COLS - 1
    oh_c = (dc[:, :, None] == jnp.arange(2 * WIN_COLS - 1)).astype(F32)
    dr = jnp.arange(WIN_ROWS)[None, :] - jnp.arange(WIN_ROWS)[:, None] + WIN_ROWS - 1
    oh_r = (dr[:, :, None] == jnp.arange(2 * WIN_ROWS - 1)).astype(F32)
    b = jnp.einsum("hij,dri,ckj->hdcrk", rpb.astype(F32), oh_r, oh_c, precision=lax.Precision.HIGHEST)
    b = jnp.where(inwin[None, None, :, None, :], b, NEG)
    return b.reshape(rpb.shape[0], WIN_ROWS, GRID_W, WIN_ROWS * GRID_W)


NA_GROUP = 16


def _na_kernel(q_ref, k_ref, v_ref, bias_ref, qn_ref, kn_ref, o_ref, qn_scr, kn_scr, *, rows):
    band = WIN_ROWS * GRID_W
    kg = kn_ref[...]
    qg = qn_ref[...] * HEAD_DIM ** -0.5

    def norm(i, carry):
        sl = pl.ds(pl.multiple_of(i * 256, 256), 256)
        kn_scr[sl, :] = (_rms(k_ref[sl, :].astype(F32), HEAD_DIM) * kg).astype(BF16)
        qn_scr[sl, :] = (_rms(q_ref[sl, :].astype(F32), HEAD_DIM) * qg).astype(BF16)
        return carry

    lax.fori_loop(0, rows * GRID_W // 256, norm, 0, unroll=2)

    def body(g, carry):
        rr = [g * NA_GROUP + i for i in range(NA_GROUP)]
        rs = [jnp.clip(r - WIN_ROWS // 2, 0, rows - WIN_ROWS) for r in rr]
        qsl = [pl.ds(pl.multiple_of(r * GRID_W, GRID_W), GRID_W) for r in rr]
        ksl = [pl.ds(pl.multiple_of(r0 * GRID_W, GRID_W), band) for r0 in rs]
        s = [lax.dot_general(qn_scr[qsl[i], :], kn_scr[ksl[i], :], (((1,), (1,)), ((), ())),
                             preferred_element_type=F32) for i in range(NA_GROUP)]
        p, l = [], []
        for i in range(NA_GROUP):
            si = s[i] + bias_ref[rr[i] - rs[i]]
            pi = jnp.exp(si - jnp.max(si, axis=-1, keepdims=True))
            l.append(jnp.sum(pi, axis=-1, keepdims=True))
            p.append(pi.astype(BF16))
        for i in range(NA_GROUP):
            o = jnp.dot(p[i], v_ref[ksl[i], :], preferred_element_type=F32) / l[i]
            o_ref[qsl[i], :] = o.astype(BF16)
        return carry

    lax.fori_loop(0, rows // NA_GROUP, body, 0)


def _na(z4, bias, qn, kn):
    _, b, s, _ = z4.shape
    rows = s // GRID_W
    assert rows >= WIN_ROWS and rows % NA_GROUP == 0 and s % 256 == 0
    slab = lambda off: pl.BlockSpec((None, None, s, LANES), lambda bi, h: (off + h, bi, 0, 0))
    return pl.pallas_call(
        functools.partial(_na_kernel, rows=rows),
        grid=(b, NA_HEADS),
        in_specs=[
            slab(QA), slab(KA), slab(VA),
            pl.BlockSpec((None, WIN_ROWS, GRID_W, WIN_ROWS * GRID_W), lambda bi, h: (h, 0, 0, 0)),
            pl.BlockSpec((1, LANES), lambda bi, h: (0, 0)),
            pl.BlockSpec((1, LANES), lambda bi, h: (0, 0)),
        ],
        out_specs=pl.BlockSpec((None, None, s, LANES), lambda bi, h: (h, bi, 0, 0)),
        out_shape=jax.ShapeDtypeStruct((NA_HEADS, b, s, LANES), BF16),
        scratch_shapes=[pltpu.VMEM((s, LANES), BF16), pltpu.VMEM((s, LANES), BF16)],
        compiler_params=_cparams(("parallel", "parallel")),
        name="mixer_a",
    )(z4, z4, z4, bias, qn, kn)


def _rope_cos_sin(seq, half):
    inv = (np.float32(ROPE_THETA) ** (-np.arange(half, dtype=np.float32) / np.float32(half))).astype(np.float32)
    ang = (np.arange(seq, dtype=np.float32)[:, None] * inv[None, :]).astype(np.float64)
    return np.cos(ang).astype(np.float32), np.sin(ang).astype(np.float32)


def _rope_tables_t(seq, half, chunk, order=None):
    cos, sin = _rope_cos_sin(seq, half)
    if order is not None:
        cos, sin = cos[order], sin[order]
    c = np.concatenate([cos, cos], axis=1).T
    sn = np.concatenate([-sin, sin], axis=1).T
    split = lambda t: jnp.asarray(t.reshape(2 * half, seq // chunk, chunk).transpose(1, 0, 2))
    return split(c), split(sn)


def _norm_rope_t(x, gt, ct, snt, comp, half):
    rows = x.shape[0]
    xt = x.T
    g = gt[:, :rows] if rows < LANES else jnp.concatenate([gt] * (rows // LANES), axis=1)
    w = 2 * half

    def one(y, gc):
        r = lax.rsqrt(jnp.sum(y * y, axis=0, keepdims=True) * (1.0 / comp) + EPS)
        y = y * r * gc
        head = y[:w]
        rot = jnp.concatenate([head[half:], head[:half]], axis=0)
        return jnp.concatenate([head * ct + rot * snt, y[w:]], axis=0)

    parts = [one(xt[o:o + comp], g[o:o + comp]) for o in range(0, LANES, comp)]
    return parts[0] if len(parts) == 1 else jnp.concatenate(parts, axis=0)


def _lane_bcast(g):
    return jnp.broadcast_to(g[:, None], (LANES, LANES))


DIL_HALF = HEAD_DIM // 4 // 2
DIL_GROUP = 8


def _dil_kernel(q0_ref, k0_ref, v0_ref, q1_ref, k1_ref, v1_ref, q2_ref, k2_ref, v2_ref,
                ct0, st0, ct1, st1, ct2, st2, qgt_ref, kgt_ref, o_ref,
                qs0, ks0, vs0, qs1, ks1, vs1, qs2, ks2, vs2, og0, og1, og2, ls0, ls1, ls2, *, seq):
    kg = kgt_ref[...]
    qg = qgt_ref[...] * HEAD_DIM ** -0.5
    groups = ((q0_ref, k0_ref, v0_ref, ct0, st0, qs0, ks0, vs0, og0, ls0),
              (q1_ref, k1_ref, v1_ref, ct1, st1, qs1, ks1, vs1, og1, ls1),
              (q2_ref, k2_ref, v2_ref, ct2, st2, qs2, ks2, vs2, og2, ls2))
    preps, attns = [], []

    for (qr, kr, vr, ct, st, qn_scr, kn_scr, v_scr, og, ls), d in zip(groups, DILATIONS):
        n = seq // d
        tq = min(128, n)
        kw = min(2 * tq, n)
        ck = min(256, n)
        nb = n // tq
        nc = n // ck

        def cls(start, size, d=d):
            return pl.ds(start, size) if d == 1 else pl.ds(start, size, stride=d)

        def prep(u, d=d, ck=ck, nc=nc, qr=qr, kr=kr, vr=vr, ct=ct, st=st,
                 qn_scr=qn_scr, kn_scr=kn_scr, v_scr=v_scr, cls=cls):
            start = u // nc + d * ck * (u % nc)
            if d == 1:
                start = pl.multiple_of(start, ck)
            src = cls(start, ck)
            dst = pl.ds(pl.multiple_of(u * ck, ck), ck)
            c, sn = ct[u], st[u]
            k = _norm_rope_t(kr[src, :].astype(F32), kg, c, sn, HEAD_DIM, DIL_HALF)
            kn_scr[dst, :] = k.T.astype(BF16)
            q = _norm_rope_t(qr[src, :].astype(F32), qg, c, sn, HEAD_DIM, DIL_HALF)
            qn_scr[dst, :] = q.T.astype(BF16)
            v_scr[dst, :] = vr[src, :].astype(BF16)

        def qblocks(gi, d=d, n=n, tq=tq, kw=kw, nb=nb, qn_scr=qn_scr, kn_scr=kn_scr, v_scr=v_scr,
                    og=og, ls=ls, cls=cls):
            uu = [gi * DIL_GROUP + i for i in range(DIL_GROUP)]
            q0 = [(u % nb) * tq for u in uu]
            ks = [jnp.clip(q - DIL_SPAN, 0, n - kw) for q in q0]
            ksl = [pl.ds(pl.multiple_of((u // nb) * n + k0, DIL_SPAN), kw) for u, k0 in zip(uu, ks)]
            s = [lax.dot_general(qn_scr[pl.ds(pl.multiple_of(u * tq, tq), tq), :], kn_scr[ksl[i], :],
                                 (((1,), (1,)), ((), ())), preferred_element_type=F32)
                 for i, u in enumerate(uu)]
            off = (lax.broadcasted_iota(jnp.int32, (tq, kw), 1) - lax.broadcasted_iota(jnp.int32, (tq, kw), 0))
            p, l, lse = [], [], []
            for i in range(DIL_GROUP):
                si = jnp.where(jnp.abs(off + (ks[i] - q0[i])) <= DIL_SPAN, s[i], NEG)
                m = jnp.max(si, axis=-1, keepdims=True)
                pi = jnp.exp(si - m)
                li = jnp.sum(pi, axis=-1, keepdims=True)
                p.append(pi.astype(BF16))
                l.append(li)
                lse.append(m + jnp.log(li))
            for i, u in enumerate(uu):
                start = u // nb + d * q0[i]
                if d == 1:
                    start = pl.multiple_of(start, tq)
                o = jnp.dot(p[i], v_scr[ksl[i], :], preferred_element_type=F32) / l[i]
                og[cls(start, tq), :] = o
                ls[cls(start, tq), :] = jnp.broadcast_to(lse[i], (tq, LANES))

        preps.append((prep, seq // ck))
        attns.append((qblocks, seq // tq // DIL_GROUP))

    def stage(prep, attn):
        if attn is None:
            fn, trips = prep
            return lax.fori_loop(0, trips, lambda u, c: (fn(u), c)[1], 0, unroll=2)
        afn, trips = attn
        if prep is None:
            return lax.fori_loop(0, trips, lambda g, c: (afn(g), c)[1], 0)
        pfn, chunks = prep
        assert chunks % trips == 0
        per = chunks // trips

        def body(g, carry):
            for i in range(per):
                pfn(g * per + i)
            afn(g)
            return carry

        return lax.fori_loop(0, trips, body, 0)

    stage(preps[0], None)
    stage(preps[1], attns[0])
    stage(preps[2], attns[1])
    stage(None, attns[2])

    def mix(i, carry):
        sl = pl.ds(pl.multiple_of(i * 256, 256), 256)
        l0, l1, l2 = ls0[sl, :], ls1[sl, :], ls2[sl, :]
        mx = jnp.maximum(jnp.maximum(l0, l1), l2)
        e0, e1, e2 = jnp.exp(l0 - mx), jnp.exp(l1 - mx), jnp.exp(l2 - mx)
        o = (e0 * og0[sl, :] + e1 * og1[sl, :] + e2 * og2[sl, :]) / (e0 + e1 + e2)
        o_ref[sl, :] = o.astype(BF16)
        return carry

    lax.fori_loop(0, seq // 256, mix, 0)


def _dil(zm4, zd4, qn, kn):
    _, b, s, _ = zm4.shape
    assert s % 256 == 0 and s // DILATIONS[-1] >= DIL_SPAN and (s // 128) % DIL_GROUP == 0
    once = dict(pipeline_mode=pl.Buffered(1))
    main = lambda off: pl.BlockSpec((None, None, s, LANES), lambda bi, hh: (off + hh, bi, 0, 0))
    dil = lambda off: pl.BlockSpec((None, None, s, LANES), lambda bi, hh: (off + hh, bi, 0, 0), **once)
    sq = pl.BlockSpec((LANES, LANES), lambda bi, hh: (0, 0))
    tabs, tab_specs = [], []
    for d in DILATIONS:
        n = s // d
        ck = min(256, n)
        flat = np.arange(s)
        tabs += list(_rope_tables_t(s, DIL_HALF, ck, order=(flat % n) * d + flat // n))
        tab_specs += [pl.BlockSpec((s // ck, 2 * DIL_HALF, ck), lambda bi, hh: (0, 0, 0), **once)] * 2
    return pl.pallas_call(
        functools.partial(_dil_kernel, seq=s),
        grid=(b, DIL_HEADS),
        in_specs=[main(B0), main(B0 + 2), main(B0 + 4),
                  dil(0), dil(2), dil(4), dil(6), dil(8), dil(10),
                  *tab_specs, sq, sq],
        out_specs=pl.BlockSpec((None, None, s, LANES), lambda bi, hh: (hh, bi, 0, 0)),
        out_shape=jax.ShapeDtypeStruct((DIL_HEADS, b, s, LANES), BF16),
        scratch_shapes=[pltpu.VMEM((s, LANES), BF16)] * 9 + [pltpu.VMEM((s, LANES), F32)] * 6,
        compiler_params=_cparams(("parallel", "parallel")),
        name="mixer_b",
    )(zm4, zm4, zm4, zd4, zd4, zd4, zd4, zd4, zd4, *tabs, _lane_bcast(qn), _lane_bcast(kn))


DIFF_HALF = DIFF_QK_DIM // 4 // 2
DIFF_TQ = 512
DIFF_TK = 256
DIFF_SAFE_BOUND = 50.0


def _diff_kernel(q_ref, k_ref, v_ref, ct_ref, snt_ref, qgt_ref, kgt_ref, lam_ref, sub_ref, o_ref,
                 kt_scr, q2_scr, bound_scr, *, seq, tq, tk, lam_init):
    @pl.when(pl.program_id(2) == 0)
    def _():
        kg = kgt_ref[...]
        qg = qgt_ref[...] * (DIFF_QK_DIM ** -0.5 * LOG2E)

        def knorm(i, carry):
            sl = pl.ds(pl.multiple_of(i * tk, tk), tk)
            kt_scr[i] = _norm_rope_t(k_ref[sl, :].astype(F32), kg, ct_ref[i], snt_ref[i],
                                     DIFF_QK_DIM, DIFF_HALF).astype(BF16)
            return carry

        lax.fori_loop(0, seq // tk, knorm, 0, unroll=4)

        per = tq // tk

        def qnorm(i, carry):
            sl = pl.ds(pl.multiple_of(i * tk, tk), tk)
            q = _norm_rope_t(q_ref[sl, :].astype(F32), qg, ct_ref[i], snt_ref[i], DIFF_QK_DIM, DIFF_HALF).T
            lo = lax.broadcasted_iota(jnp.int32, q.shape, 1) < DIFF_QK_DIM
            blk = i // per
            off = (i % per) * tk
            q2_scr[blk, pl.ds(pl.multiple_of(off, tk), tk), :] = jnp.where(lo, q, 0.0).astype(BF16)
            q2_scr[blk, pl.ds(pl.multiple_of(tq + off, tk), tk), :] = jnp.where(lo, 0.0, q).astype(BF16)
            return carry

        lax.fori_loop(0, seq // tk, qnorm, 0, unroll=4)

        gmax = jnp.max(jnp.abs(qg)) * jnp.max(jnp.abs(kg))
        bound_scr[0] = gmax * (DIFF_QK_DIM * 1.01)

    nk = seq // tk
    lane_sums = lambda p: functools.reduce(jnp.add, [p[:, c * LANES:(c + 1) * LANES] for c in range(tk // LANES)])

    def finish(acc, l):
        lv = lam_ref[...]
        lam = (jnp.exp(jnp.sum(lv[0:1] * lv[1:2], axis=-1, keepdims=True))
               - jnp.exp(jnp.sum(lv[2:3] * lv[3:4], axis=-1, keepdims=True)) + lam_init)
        on = acc / jnp.sum(l, axis=-1, keepdims=True)
        o = on[:tq] - lam * on[tq:]
        o_ref[...] = (_rms(o, HEAD_DIM) * sub_ref[...] * (1.0 - lam_init)).astype(BF16)

    def attend(shift):
        q2 = q2_scr[pl.program_id(2)]
        m = jnp.full((2 * tq, 1), -jnp.inf, F32)
        l = jnp.zeros((2 * tq, LANES), F32)
        acc = jnp.zeros((2 * tq, HEAD_DIM), F32)
        s_next = jnp.dot(q2, kt_scr[0], preferred_element_type=F32)
        for j in range(nk):
            s = s_next
            if j + 1 < nk:
                s_next = jnp.dot(q2, kt_scr[j + 1], preferred_element_type=F32)
            v = v_ref[j * tk:(j + 1) * tk, :]
            if shift is None:
                m_new = jnp.maximum(m, jnp.max(s, axis=-1, keepdims=True))
                alpha = jnp.exp2(m - m_new)
                p = jnp.exp2(s - m_new)
                l = alpha * l + lane_sums(p)
                acc = alpha * acc + jnp.dot(p.astype(BF16), v, preferred_element_type=F32)
                m = m_new
            else:
                p = jnp.exp2(s - shift)
                l = l + lane_sums(p)
                acc = acc + jnp.dot(p.astype(BF16), v, preferred_element_type=F32)
        finish(acc, l)

    bound = bound_scr[0]
    safe = bound <= DIFF_SAFE_BOUND

    @pl.when(safe)
    def _():
        attend(bound)

    @pl.when(jnp.logical_not(safe))
    def _():
        attend(None)


def _diff(z4, qn, kn, lamv, sub, lam_init):
    _, b, s, _ = z4.shape
    tq = min(DIFF_TQ, s)
    tk = min(DIFF_TK, s)
    assert s % tq == 0 and tq % tk == 0 and tk % LANES == 0
    tabs = _rope_tables_t(s, DIFF_HALF, tk)
    gt = lambda g: _lane_bcast(jnp.concatenate([g, g]))
    kv = lambda off: pl.BlockSpec((None, None, s, LANES), lambda bi, h, qi: (off + h, bi, 0, 0))
    tab = pl.BlockSpec((s // tk, 2 * DIFF_HALF, tk), lambda bi, h, qi: (0, 0, 0))
    sq = pl.BlockSpec((LANES, LANES), lambda bi, h, qi: (0, 0))
    vec = pl.BlockSpec((1, LANES), lambda bi, h, qi: (0, 0))
    return pl.pallas_call(
        functools.partial(_diff_kernel, seq=s, tq=tq, tk=tk, lam_init=lam_init),
        grid=(b, DIFF_HEADS, s // tq),
        in_specs=[
            kv(QC), kv(KC), kv(VC), tab, tab, sq, sq,
            pl.BlockSpec((4, DIFF_QK_DIM), lambda bi, h, qi: (0, 0)),
            vec,
        ],
        out_specs=pl.BlockSpec((None, None, tq, LANES), lambda bi, h, qi: (h, bi, qi, 0)),
        out_shape=jax.ShapeDtypeStruct((DIFF_HEADS, b, s, LANES), BF16),
        scratch_shapes=[pltpu.VMEM((s // tk, LANES, tk), BF16), pltpu.VMEM((s // tq, 2 * tq, LANES), BF16),
                        pltpu.SMEM((1,), F32)],
        compiler_params=_cparams(("parallel", "parallel", "arbitrary")),
        name="mixer_c",
    )(z4, z4, z4, *tabs, gt(qn), gt(kn), lamv, sub)


def _merge_kernel(x_ref, g_ref, sc_ref, sh_ref, oa_ref, ob_ref, oc_ref, wa_ref, wb_ref, wc_ref,
                  wga_ref, wgb_ref, wgc_ref, bga_ref, bgb_ref, bgc_ref, m_ref, h_scr):
    @pl.when(pl.program_id(1) == 0)
    def _():
        _modnorm_to(h_scr, x_ref, g_ref, sc_ref, sh_ref)

    h = h_scr[...]
    cat = lambda ref: jnp.concatenate([ref[i] for i in range(ref.shape[0])], axis=-1)

    def gate(w_ref, b_ref):
        t = jnp.dot(h, w_ref[...], preferred_element_type=F32) + b_ref[...]
        return 1.0 / (1.0 + jnp.exp(-t))

    y = gate(wga_ref, bga_ref) * jnp.dot(cat(oa_ref), wa_ref[...], preferred_element_type=F32)
    y = y + gate(wgb_ref, bgb_ref) * jnp.dot(cat(ob_ref), wb_ref[...], preferred_element_type=F32)
    y = y + gate(wgc_ref, bgc_ref) * jnp.dot(cat(oc_ref), wc_ref[...], preferred_element_type=F32)
    m_ref[...] = y.astype(BF16)


def _merge(x2, seq, g, sc, sh, oa, ob, oc, wa, wb, wc, wg, bg):
    t, d = x2.shape
    tm = _pick(seq, 1024)
    tn = _pick(d, 512)
    per = seq // tm
    nj = d // tn
    heads = lambda nh: pl.BlockSpec((nh, tm, LANES), lambda i, j: (0, i, 0))
    wcol = lambda k: pl.BlockSpec((k, tn), lambda i, j: (0, j))
    gcol = lambda sec: pl.BlockSpec((d, tn), lambda i, j: (0, sec * nj + j))
    bcol = lambda sec: pl.BlockSpec((1, tn), lambda i, j: (0, sec * nj + j))
    modspec = pl.BlockSpec((None, 1, d), lambda i, j: (i // per, 0, 0))
    bg2 = bg.reshape(1, 3 * d)
    return pl.pallas_call(
        _merge_kernel,
        grid=(t // tm, nj),
        in_specs=[
            pl.BlockSpec((tm, d), lambda i, j: (i, 0)), pl.BlockSpec((1, d), lambda i, j: (0, 0)),
            modspec, modspec,
            heads(NA_HEADS), heads(DIL_HEADS), heads(DIFF_HEADS),
            wcol(wa.shape[0]), wcol(wb.shape[0]), wcol(wc.shape[0]),
            gcol(0), gcol(1), gcol(2), bcol(0), bcol(1), bcol(2),
        ],
        out_specs=pl.BlockSpec((tm, tn), lambda i, j: (i, j)),
        out_shape=jax.ShapeDtypeStruct((t, d), BF16),
        scratch_shapes=[pltpu.VMEM((tm, d), BF16)],
        compiler_params=_cparams(("parallel", "arbitrary")),
        name="merge",
    )(x2, g, sc, sh, oa, ob, oc, wa, wb, wc, wg, wg, wg, bg2, bg2, bg2)


def _outproj_kernel(m_ref, w_ref, x_ref, g1_ref, o_ref):
    y = jnp.dot(m_ref[...], w_ref[...], preferred_element_type=F32)
    o_ref[...] = x_ref[...] + g1_ref[...] * y


def _outproj(m, w, x2, seq, g1):
    t, d = x2.shape
    tm = _pick(seq, 1024)
    tn = _pick(d, 1024)
    per = seq // tm
    return pl.pallas_call(
        _outproj_kernel,
        grid=(t // tm, d // tn),
        in_specs=[
            pl.BlockSpec((tm, d), lambda i, j: (i, 0)),
            pl.BlockSpec((d, tn), lambda i, j: (0, j)),
            pl.BlockSpec((tm, tn), lambda i, j: (i, j)),
            pl.BlockSpec((None, 1, tn), lambda i, j: (i // per, 0, j)),
        ],
        out_specs=pl.BlockSpec((tm, tn), lambda i, j: (i, j)),
        out_shape=jax.ShapeDtypeStruct((t, d), F32),
        compiler_params=_cparams(("parallel", "parallel")),
        name="out_proj",
    )(m, w, x2, g1)


MLP_ACC_COLS = 512


def _mlp_kernel(x_ref, g_ref, sc_ref, sh_ref, g2_ref, w1_ref, w2_ref, o_ref, h_scr):
    k = pl.program_id(1)
    last = pl.num_programs(1) - 1
    d = o_ref.shape[1]
    nc = _pick(d, MLP_ACC_COLS)

    def step(first, final):
        u = jnp.maximum(jnp.dot(h_scr[...], w1_ref[...], preferred_element_type=F32), 0.0)
        u = (u * u).astype(BF16)
        for c in range(d // nc):
            cs = slice(c * nc, (c + 1) * nc)
            f = jnp.dot(u, w2_ref[:, cs], preferred_element_type=F32)
            if not first:
                f = o_ref[:, cs] + f
            if final:
                f = x_ref[:, cs] + g2_ref[:, cs] * f
            o_ref[:, cs] = f

    @pl.when(k == 0)
    def _():
        _modnorm_to(h_scr, x_ref, g_ref, sc_ref, sh_ref)
        step(True, False)

    @pl.when(jnp.logical_and(k > 0, k < last))
    def _():
        step(False, False)

    @pl.when(k == last)
    def _():
        step(False, True)


def _mlp(x2, seq, g, sc, sh, g2, w1, w2):
    t, d = x2.shape
    ff = w1.shape[1]
    tm = _pick(seq, 1024)
    tf = _pick(ff, 512)
    assert ff // tf >= 2
    per = seq // tm
    modspec = pl.BlockSpec((None, 1, d), lambda i, k: (i // per, 0, 0))
    return pl.pallas_call(
        _mlp_kernel,
        grid=(t // tm, ff // tf),
        in_specs=[
            pl.BlockSpec((tm, d), lambda i, k: (i, 0)),
            pl.BlockSpec((1, d), lambda i, k: (0, 0)),
            modspec, modspec, modspec,
            pl.BlockSpec((d, tf), lambda i, k: (0, k)),
            pl.BlockSpec((tf, d), lambda i, k: (k, 0)),
        ],
        out_specs=pl.BlockSpec((tm, d), lambda i, k: (i, 0)),
        out_shape=jax.ShapeDtypeStruct((t, d), F32),
        scratch_shapes=[pltpu.VMEM((tm, d), BF16)],
        compiler_params=_cparams(("parallel", "arbitrary")),
        name="mlp",
    )(x2, g, sc, sh, g2, w1, w2)


def _layer(x2, b, s, mods, layer_idx, p):
    d = x2.shape[1]
    sh1, sc1, g1, sh2, sc2, g2 = [m.reshape(b, 1, d) for m in jnp.split(mods, 6, axis=-1)]
    zm = _inproj(x2, s, p["norm1_g"], sc1, sh1, p["w_in_main"], BF16, "in_proj_main")
    zd = _inproj(x2, s, p["norm1_g"], sc1, sh1, p["w_in_dil"], F32, "in_proj_dil")
    zm4 = zm.reshape(zm.shape[0], b, s, LANES)
    zd4 = zd.reshape(zd.shape[0], b, s, LANES)
    oa = _na(zm4, p["bias_a"], p["qn_a"], p["kn_a"]).reshape(NA_HEADS, b * s, LANES)
    ob = _dil(zm4, zd4, p["qn_b"], p["kn_b"]).reshape(DIL_HEADS, b * s, LANES)
    lam_init = 0.8 - 0.6 * math.exp(-0.3 * layer_idx)
    oc = _diff(zm4, p["qn_c"], p["kn_c"], p["lamv"], p["subln_c"], lam_init)
    oc = oc.reshape(DIFF_HEADS, b * s, LANES)
    m = _merge(x2, s, p["norm1_g"], sc1, sh1, oa, ob, oc,
               p["w_br_a"], p["w_br_b"], p["w_br_c"], p["w_gate"], p["b_gate"])
    x2 = _outproj(m, p["w_out"], x2, s, g1)
    return _mlp(x2, s, p["norm2_g"], sc2, sh2, g2, p["w_ff1"], p["w_ff2"])


def kernel(x_prompt, x_sample, c_prompt, c_sample, norm1_g, w_ada, b_ada, w_in, qn_a, kn_a, rpb_a, qn_b, kn_b, qn_c, kn_c, lam_q1, lam_k1, lam_q2, lam_k2, subln_c, w_br_a, w_br_b, w_br_c, w_gate, b_gate, w_out, norm2_g, w_ff1, w_ff2):
    depth = w_in.shape[0]
    bp, sp, d = x_prompt.shape
    bs, ss, _ = x_sample.shape
    mods = _ada(jnp.concatenate([c_prompt, c_sample], axis=0), w_ada, b_ada)

    xp = x_prompt.reshape(bp * sp, d)
    xs = x_sample.reshape(bs * ss, d)
    for l in range(depth):
        w_in_main, w_in_dil = _split_w_in(w_in[l].astype(BF16))
        p = {
            "norm1_g": norm1_g[l].reshape(1, d), "norm2_g": norm2_g[l].reshape(1, d),
            "w_in_main": w_in_main, "w_in_dil": w_in_dil,
            "w_gate": w_gate[l].astype(BF16), "b_gate": b_gate[l],
            "w_br_a": w_br_a[l].astype(BF16), "w_br_b": w_br_b[l].astype(BF16),
            "w_br_c": w_br_c[l].astype(BF16), "w_out": w_out[l].astype(BF16),
            "w_ff1": w_ff1[l].astype(BF16), "w_ff2": w_ff2[l].astype(BF16),
            "qn_a": qn_a[l].reshape(1, LANES), "kn_a": kn_a[l].reshape(1, LANES),
            "qn_b": qn_b[l], "kn_b": kn_b[l],
            "qn_c": qn_c[l], "kn_c": kn_c[l],
            "lamv": jnp.stack([lam_q1[l], lam_k1[l], lam_q2[l], lam_k2[l]]).astype(F32),
            "subln_c": subln_c[l].reshape(1, LANES),
            "bias_a": _na_bias(rpb_a[l]),
        }
        xp = _layer(xp, bp, sp, mods[l, :bp], l, p)
        xs = _layer(xs, bs, ss, mods[l, bp:], l, p)
    return xp.reshape(bp, sp, d), xs.reshape(bs, ss, d)
```
